```python
import math
import jax
import jax.numpy as jnp
from jax import lax
import numpy as np

D_MODEL = 1024
BATCH = 1
SEQ = 16384
DEPTH = 2
DEC_BATCH = 32
DEC_SEQ = 1
PAST_LEN = 16384
PAGE_SIZE = 128

N_A_LAYERS = DEPTH // 2
N_B_LAYERS = DEPTH - N_A_LAYERS
CHUNK = 128
A_GROUPS = 8
A_WIDTH = 2 * D_MODEL
A_GROUP_DIM = A_WIDTH // A_GROUPS
N_HEADS = 8
HEAD_DIM = D_MODEL // N_HEADS // 2
V_DIM = 2 * HEAD_DIM
ATTN_SCALE = HEAD_DIM ** -0.5
Q_BLOCK = 128
N_BUCKETS = 32
MAX_EXACT = N_BUCKETS // 2
MAX_DISTANCE = 128
N_EXPERTS = 32
TOP_K = 4
D_EXPERT = D_MODEL
SWIGLU_LIMIT = 7.0
SWIGLU_ALPHA = 1.702
EXPERT_BLOCK = 128
EPS = 1e-6

kernel_name = 'yoco_chunk_gmlp_diff_attn_moe_step'


def _rms_norm(x, g):
    xf = x.astype(jnp.float32)
    y = xf * lax.rsqrt(jnp.mean(xf * xf, axis=-1, keepdims=True) + EPS)
    return (y * g.astype(jnp.float32)).astype(x.dtype)


def _layer_norm(x, g, b):
    xf = x.astype(jnp.float32)
    mu = jnp.mean(xf, axis=-1, keepdims=True)
    xc = xf - mu
    y = xc * lax.rsqrt(jnp.mean(xc * xc, axis=-1, keepdims=True) + EPS)
    return (y * g.astype(jnp.float32) + b.astype(jnp.float32)).astype(x.dtype)


def _rel_bucket(rel):
    n = jnp.maximum(rel, 0)
    nf = jnp.maximum(n, 1).astype(jnp.float32)
    large = MAX_EXACT + (jnp.log(nf / MAX_EXACT) / math.log(MAX_DISTANCE / MAX_EXACT)
                         * (N_BUCKETS - MAX_EXACT)).astype(jnp.int32)
    return jnp.where(n < MAX_EXACT, n, jnp.minimum(large, N_BUCKETS - 1))


def _rel_bias(rel, rel_table):
    b = rel_table.astype(jnp.float32)[_rel_bucket(rel)]
    return jnp.repeat(jnp.moveaxis(b, -1, 0), 2, axis=0)


def _diff_combine(s, lam):
    p = jax.nn.softmax(s, axis=-1)
    p = p.reshape(p.shape[0], N_HEADS, 2, p.shape[2], p.shape[3])
    return p[:, :, 0] - lam * p[:, :, 1]


def _attend_prompt(q, k, v, lam, rel_table):
    B, S = q.shape[0], q.shape[1]
    qb = min(Q_BLOCK, S)
    nqb = S // qb
    q_blocks = jnp.moveaxis(q.reshape(B, nqb, qb, 2 * N_HEADS, HEAD_DIM), 1, 0)
    k_pos = jnp.arange(S, dtype=jnp.int32)

    def block(args):
        q_blk, i = args
        rel = (i * qb + jnp.arange(qb, dtype=jnp.int32))[:, None] - k_pos[None, :]
        s = jnp.einsum('bqhd,bkhd->bhqk', q_blk, k).astype(jnp.float32) * ATTN_SCALE + _rel_bias(rel, rel_table)
        s = jnp.where(rel >= 0, s, -jnp.inf)
        return jnp.einsum('bhqk,bkhd->bqhd', _diff_combine(s, lam), v)

    o = lax.map(block, (q_blocks, jnp.arange(nqb, dtype=jnp.int32)))
    return jnp.moveaxis(o, 0, 1).reshape(B, S, N_HEADS, V_DIM).astype(q.dtype)


def _attend_sample(q, k_new, v_new, lam, rel_table, cache_k, cache_v, page_table):
    NB, T = q.shape[0], q.shape[1]
    past = page_table.shape[1] * cache_k.shape[1]
    k_past = cache_k[page_table].reshape(NB, past, 2 * N_HEADS, HEAD_DIM)
    v_past = cache_v[page_table].reshape(NB, past, N_HEADS, V_DIM)
    t_new = jnp.arange(T, dtype=jnp.int32)
    rel_past = (past + t_new)[:, None] - jnp.arange(past, dtype=jnp.int32)[None, :]
    rel_new = t_new[:, None] - t_new[None, :]
    s_past = jnp.einsum('bqhd,bkhd->bhqk', q, k_past).astype(jnp.float32) * ATTN_SCALE + _rel_bias(rel_past, rel_table)
    s_new = jnp.einsum('bqhd,bkhd->bhqk', q, k_new).astype(jnp.float32) * ATTN_SCALE + _rel_bias(rel_new, rel_table)
    s_new = jnp.where(rel_new >= 0, s_new, -jnp.inf)
    pd = _diff_combine(jnp.concatenate([s_past, s_new], axis=-1), lam)
    o = (jnp.einsum('bhqk,bkhd->bqhd', pd[..., :past], v_past)
         + jnp.einsum('bhqk,bkhd->bqhd', pd[..., past:], v_new))
    return o.astype(q.dtype)


def _chunk_gmlp(h, w_in, b_in, g_v, b_v, w_s, b_s, w_out):
    B, S, _ = h.shape
    z = jax.nn.gelu(h @ w_in + b_in, approximate=False)
    u, v = jnp.split(z, 2, axis=-1)
    v = _layer_norm(v, g_v, b_v)
    L = min(S, CHUNK)
    vg = v.reshape(B, S // L, L, A_GROUPS, A_GROUP_DIM)
    wm = jnp.tril(w_s)[:, :L, :L]
    s = jnp.einsum('gts,bcsgd->bctgd', wm, vg) + b_s[:, :L].T[None, None, :, :, None]
    return (u * s.reshape(B, S, A_WIDTH)) @ w_out, v


def _moe(x, w_r, b_r, w_gu, b_gu, w_dn, b_dn):
    B, S, D = x.shape
    xt = x.reshape(B * S, D)
    T = xt.shape[0]
    logits = (xt @ w_r + b_r).astype(jnp.float32)
    top_v, top_e = lax.top_k(logits, TOP_K)
    gates = jax.nn.softmax(top_v, axis=-1)
    A = T * TOP_K
    e_flat = top_e.reshape(A)
    g_flat = gates.reshape(A)
    tok_flat = jnp.arange(A, dtype=jnp.int32) // TOP_K
    order = jnp.argsort(e_flat)
    e_s = e_flat[order]
    tok_s = tok_flat[order]
    g_s = g_flat[order]
    counts = jnp.bincount(e_flat, length=N_EXPERTS)
    start = jnp.cumsum(counts) - counts
    pcounts = (counts + EXPERT_BLOCK - 1) // EXPERT_BLOCK * EXPERT_BLOCK
    pend = jnp.cumsum(pcounts)
    pstart = pend - pcounts
    dest = pstart[e_s] + (jnp.arange(A, dtype=jnp.int32) - start[e_s])
    n_blocks = -(-A // EXPERT_BLOCK) + N_EXPERTS
    R = n_blocks * EXPERT_BLOCK
    row_tok = jnp.full((R,), T, jnp.int32).at[dest].set(tok_s)
    row_gate = jnp.zeros((R,), jnp.float32).at[dest].set(g_s)
    blk_e = jnp.minimum(jnp.searchsorted(pend, jnp.arange(n_blocks, dtype=jnp.int32) * EXPERT_BLOCK, side='right'),
                        N_EXPERTS - 1)
    x_rows = jnp.concatenate([xt, jnp.zeros((1, D), xt.dtype)], axis=0)[row_tok].reshape(n_blocks, EXPERT_BLOCK, D)

    def expert_block(args):
        xb, e = args
        hgu = xb @ w_gu[e] + b_gu[e]
        gate = jnp.minimum(hgu[:, :D_EXPERT], SWIGLU_LIMIT)
        up = jnp.clip(hgu[:, D_EXPERT:], -SWIGLU_LIMIT, SWIGLU_LIMIT)
        act = gate * jax.nn.sigmoid(SWIGLU_ALPHA * gate) * (up + 1.0)
        return act @ w_dn[e] + b_dn[e]

    y_rows = lax.map(expert_block, (x_rows, blk_e)).reshape(R, D)
    y = jax.ops.segment_sum(y_rows * row_gate[:, None].astype(y_rows.dtype), row_tok, num_segments=T + 1)[:T]
    return y.reshape(B, S, D).astype(x.dtype)


def _trunk(x, attend, p):
    B, S, _ = x.shape
    chunk_v = []
    k = None
    v = None
    for l in range(DEPTH):
        if l == N_A_LAYERS:
            hk = _rms_norm(x, p['g_kv'])
            k = (hk @ p['w_k']).reshape(B, S, 2 * N_HEADS, HEAD_DIM)
            v = (hk @ p['w_v']).reshape(B, S, N_HEADS, V_DIM)
        h = _rms_norm(x, p['g_mix'][l])
        if l < N_A_LAYERS:
            mix, v_rows = _chunk_gmlp(h, p['a_w_in'][l], p['a_b_in'][l], p['a_g_v'][l], p['a_b_v'][l],
                                      p['a_w_s'][l], p['a_b_s'][l], p['a_w_out'][l])
            chunk_v.append(v_rows)
        else:
            j = l - N_A_LAYERS
            lam_init = 0.8 - 0.6 * math.exp(-0.3 * l)
            q = (h @ p['b_w_q'][j]).reshape(B, S, 2 * N_HEADS, HEAD_DIM)
            lam = (jnp.exp(jnp.sum(p['b_lq1'][j].astype(jnp.float32) * p['b_lk1'][j].astype(jnp.float32)))
                   - jnp.exp(jnp.sum(p['b_lq2'][j].astype(jnp.float32) * p['b_lk2'][j].astype(jnp.float32)))
                   + lam_init)
            o = _rms_norm(attend(q, k, v, lam), p['b_g_sub'][j]) * (1.0 - lam_init)
            mix = o.reshape(B, S, N_HEADS * V_DIM) @ p['b_w_o'][j]
        x = x + mix
        x = x + _moe(_rms_norm(x, p['g_ffn'][l]), p['moe_w_router'][l], p['moe_b_router'][l],
                     p['moe_w_gu'][l], p['moe_b_gu'][l], p['moe_w_down'][l], p['moe_b_down'][l])
    return _rms_norm(x, p['g_out']), k, v, jnp.stack(chunk_v)


def setup_inputs(seed: int = 0) -> dict:
    key = jax.random.key(seed)
    ks = iter(jax.random.split(key, 40))

    def nrm(shape, scale):
        return jax.random.normal(next(ks), shape, jnp.float32) * scale

    n_pages = PAST_LEN // PAGE_SIZE
    n_phys = (5 * DEC_BATCH * n_pages + 3) // 4
    x_prompt = nrm((BATCH, SEQ, D_MODEL), 1.0)
    x_sample = nrm((DEC_BATCH, DEC_SEQ, D_MODEL), 1.0)
    cache_k = nrm((n_phys, PAGE_SIZE, 2 * N_HEADS, HEAD_DIM), 1.0)
    cache_v = nrm((n_phys, PAGE_SIZE, N_HEADS, V_DIM), 1.0)
    page_table = jax.random.permutation(next(ks), n_phys)[:DEC_BATCH * n_pages].reshape(DEC_BATCH, n_pages).astype(jnp.int32)
    return {
        'x_prompt': x_prompt,
        'x_sample': x_sample,
        'cache_k': cache_k,
        'cache_v': cache_v,
        'page_table': page_table,
        'g_mix': 1.0 + nrm((DEPTH, D_MODEL), 0.1),
        'g_ffn': 1.0 + nrm((DEPTH, D_MODEL), 0.1),
        'g_kv': 1.0 + nrm((D_MODEL,), 0.1),
        'g_out': 1.0 + nrm((D_MODEL,), 0.1),
        'a_w_in': nrm((N_A_LAYERS, D_MODEL, 2 * A_WIDTH), D_MODEL ** -0.5),
        'a_b_in': nrm((N_A_LAYERS, 2 * A_WIDTH), 0.02),
        'a_g_v': 1.0 + nrm((N_A_LAYERS, A_WIDTH), 0.1),
        'a_b_v': nrm((N_A_LAYERS, A_WIDTH), 0.02),
        'a_w_s': nrm((N_A_LAYERS, A_GROUPS, CHUNK, CHUNK), CHUNK ** -0.5),
        'a_b_s': 1.0 + nrm((N_A_LAYERS, A_GROUPS, CHUNK), 0.1),
        'a_w_out': nrm((N_A_LAYERS, A_WIDTH, D_MODEL), A_WIDTH ** -0.5),
        'w_k': nrm((D_MODEL, 2 * N_HEADS * HEAD_DIM), D_MODEL ** -0.5),
        'w_v': nrm((D_MODEL, N_HEADS * V_DIM), D_MODEL ** -0.5),
        'b_w_q': nrm((N_B_LAYERS, D_MODEL, 2 * N_HEADS * HEAD_DIM), D_MODEL ** -0.5),
        'b_lq1': nrm((N_B_LAYERS, HEAD_DIM), 0.1),
        'b_lk1': nrm((N_B_LAYERS, HEAD_DIM), 0.1),
        'b_lq2': nrm((N_B_LAYERS, HEAD_DIM), 0.1),
        'b_lk2': nrm((N_B_LAYERS, HEAD_DIM), 0.1),
        'b_g_sub': 1.0 + nrm((N_B_LAYERS, V_DIM), 0.1),
        'b_w_o': nrm((N_B_LAYERS, N_HEADS * V_DIM, D_MODEL), (N_HEADS * V_DIM) ** -0.5),
        'rel_table': nrm((N_BUCKETS, N_HEADS), 0.5),
        'moe_w_router': nrm((DEPTH, D_MODEL, N_EXPERTS), D_MODEL ** -0.5),
        'moe_b_router': nrm((DEPTH, N_EXPERTS), 0.01),
        'moe_w_gu': nrm((DEPTH, N_EXPERTS, D_MODEL, 2 * D_EXPERT), D_MODEL ** -0.5),
        'moe_b_gu': nrm((DEPTH, N_EXPERTS, 2 * D_EXPERT), 0.02),
        'moe_w_down': nrm((DEPTH, N_EXPERTS, D_EXPERT, D_MODEL), D_EXPERT ** -0.5),
        'moe_b_down': nrm((DEPTH, N_EXPERTS, D_MODEL), 0.02),
    }


def reference(x_prompt, x_sample, cache_k, cache_v, page_table, g_mix, g_ffn, g_kv, g_out,
              a_w_in, a_b_in, a_g_v, a_b_v, a_w_s, a_b_s, a_w_out, w_k, w_v, b_w_q,
              b_lq1, b_lk1, b_lq2, b_lk2, b_g_sub, b_w_o, rel_table,
              moe_w_router, moe_b_router, moe_w_gu, moe_b_gu, moe_w_down, moe_b_down):
    p = {
        'g_mix': g_mix, 'g_ffn': g_ffn, 'g_kv': g_kv, 'g_out': g_out,
        'a_w_in': a_w_in, 'a_b_in': a_b_in, 'a_g_v': a_g_v, 'a_b_v': a_b_v,
        'a_w_s': a_w_s, 'a_b_s': a_b_s, 'a_w_out': a_w_out,
        'w_k': w_k, 'w_v': w_v, 'b_w_q': b_w_q,
        'b_lq1': b_lq1, 'b_lk1': b_lk1, 'b_lq2': b_lq2, 'b_lk2': b_lk2,
        'b_g_sub': b_g_sub, 'b_w_o': b_w_o,
        'moe_w_router': moe_w_router, 'moe_b_router': moe_b_router,
        'moe_w_gu': moe_w_gu, 'moe_b_gu': moe_b_gu,
        'moe_w_down': moe_w_down, 'moe_b_down': moe_b_down,
    }

    def attend_prompt(q, k, v, lam):
        return _attend_prompt(q, k, v, lam, rel_table)

    def attend_sample(q, k, v, lam):
        return _attend_sample(q, k, v, lam, rel_table, cache_k, cache_v, page_table)

    y_prompt, k_prompt, v_prompt, _ = _trunk(x_prompt, attend_prompt, p)
    y_sample, k_sample, v_sample, chunk_v_sample = _trunk(x_sample, attend_sample, p)
    return (y_prompt, y_sample, k_prompt, v_prompt, k_sample, v_sample, chunk_v_sample)
```

```python
import functools
import math

import numpy as np
import jax
import jax.numpy as jnp
from jax import lax
from jax.experimental import pallas as pl
from jax.experimental.pallas import tpu as pltpu

F32 = jnp.float32
BF16 = jnp.bfloat16

EPS = 1e-6
CHUNK = 128
A_GROUPS = 8
N_HEADS = 8
HEAD_DIM = 64
V_DIM = 2 * HEAD_DIM
ATTN_SCALE = HEAD_DIM ** -0.5
N_BUCKETS = 32
MAX_EXACT = N_BUCKETS // 2
MAX_DISTANCE = 128
TOP_K = 4
SWIGLU_LIMIT = 7.0
SWIGLU_ALPHA = 1.702
N_A_LAYERS = 1

V7X_LANES = 128
V7X_SUBLANES = 8
V7X_VMEM_LIMIT_BYTES = 56 * 1024 * 1024

NEG_BIG = -1e30
PROMPT_TOKEN_TILE = 256
PROMPT_EXPERT_BLOCK = 256
ATTN_BLOCK = 256
SAMPLE_EXPERT_BLOCK = 8
PAGES_PER_STEP = 4


def _cparams(n_axes):
    return pltpu.CompilerParams(
        dimension_semantics=("arbitrary",) * n_axes,
        vmem_limit_bytes=V7X_VMEM_LIMIT_BYTES,
    )


def _mm(a, w, precise):
    a_hi = a.astype(BF16)
    w_hi = w.astype(BF16)
    out = jnp.dot(a_hi, w_hi, preferred_element_type=F32)
    if precise:
        a_lo = (a - a_hi.astype(F32)).astype(BF16)
        w_lo = (w.astype(F32) - w_hi.astype(F32)).astype(BF16)
        out = out + jnp.dot(a_lo, w_hi, preferred_element_type=F32)
        out = out + jnp.dot(a_hi, w_lo, preferred_element_type=F32)
    return out


def _rms(x, g):
    return x * lax.rsqrt(jnp.mean(x * x, axis=-1, keepdims=True) + EPS) * g


def _gelu(x):
    return 0.5 * x * (1.0 + lax.erf(x * (2.0 ** -0.5)))


def _cols4(c0, c1, c2, c3):
    m = c0.shape[0]
    lane = lax.broadcasted_iota(jnp.int32, (m, TOP_K), 1)
    return jnp.where(lane == 0, c0, jnp.where(lane == 1, c1, jnp.where(lane == 2, c2, c3)))


def _router_tail(x1, gffn, wr, br, precise, runcnt_ref, xn_ref, tope_ref, gate_ref, rank_ref, cnt_ref):
    m = x1.shape[0]
    xn = _rms(x1, gffn)
    xn_ref[...] = xn
    logits = _mm(xn, wr, precise) + br
    n_e = logits.shape[1]
    lane = lax.broadcasted_iota(jnp.int32, (m, n_e), 1).astype(F32)
    vals, sels, ohs = [], [], []
    cur = logits
    for _ in range(TOP_K):
        mx = jnp.max(cur, axis=-1, keepdims=True)
        sel = jnp.min(jnp.where(cur == mx, lane, float(n_e)), axis=-1, keepdims=True)
        oh = lane == sel
        vals.append(mx)
        sels.append(sel)
        ohs.append(oh.astype(F32))
        cur = jnp.where(oh, -jnp.inf, cur)
    es = [jnp.exp(v - vals[0]) for v in vals]
    den = es[0] + es[1] + es[2] + es[3]
    gate_ref[...] = _cols4(*[e / den for e in es])
    tope_ref[...] = _cols4(*sels).astype(jnp.int32)
    row = lax.broadcasted_iota(jnp.int32, (m, m), 0)
    col = lax.broadcasted_iota(jnp.int32, (m, m), 1)
    ltri = (row > col).astype(BF16)
    prior = runcnt_ref[...]
    ranks = []
    for oh in ohs:
        within = jnp.dot(ltri, oh.astype(BF16), preferred_element_type=F32)
        ranks.append(jnp.sum((prior + within) * oh, axis=-1, keepdims=True))
        prior = prior + jnp.sum(oh, axis=0, keepdims=True)
    rank_ref[...] = _cols4(*ranks).astype(jnp.int32)
    runcnt_ref[...] = prior
    cnt_ref[...] = prior.astype(jnp.int32)


def _gather_combine(dest_ref, y_hbm, ybuf, sem, gates, tm):
    def issue(t, carry):
        for k in range(TOP_K):
            d = dest_ref[0, 0, t * TOP_K + k]
            pltpu.make_async_copy(y_hbm.at[pl.ds(d, 1)], ybuf.at[k, pl.ds(t, 1)], sem).start()
        return carry
    lax.fori_loop(0, tm, issue, 0)
    for k in range(TOP_K):
        pltpu.make_async_copy(y_hbm.at[pl.ds(0, tm)], ybuf.at[k], sem).wait()
    acc = gates[:, 0:1] * ybuf[0]
    for k in range(1, TOP_K):
        acc = acc + gates[:, k:k + 1] * ybuf[k]
    return acc


def _mixer_a_kernel(*refs, chunked, precise, want_v):
    (x_ref, gmix_ref, win_ref, bin_ref, gv_ref, bv_ref, ws_ref, bs_ref, wout_ref,
     gffn_ref, wr_ref, br_ref) = refs[:12]
    rest = list(refs[12:])
    x1_ref, xn_ref = rest[0], rest[1]
    rest = rest[2:]
    v_ref = rest.pop(0) if want_v else None
    tope_ref, gate_ref, rank_ref, cnt_ref, runcnt_ref = rest[:5]
    us_ref = rest[5] if chunked else None

    @pl.when(pl.program_id(0) == 0)
    def _():
        runcnt_ref[...] = jnp.zeros_like(runcnt_ref)

    x = x_ref[...]
    tm = x.shape[0]
    aw = wout_ref.shape[0]
    h = _rms(x, gmix_ref[...])
    u = _gelu(_mm(h, win_ref[:, :aw], precise) + bin_ref[:, :aw])
    vr = _gelu(_mm(h, win_ref[:, aw:], precise) + bin_ref[:, aw:])
    mu = jnp.mean(vr, axis=-1, keepdims=True)
    vc = vr - mu
    v = vc * lax.rsqrt(jnp.mean(vc * vc, axis=-1, keepdims=True) + EPS) * gv_ref[...] + bv_ref[...]
    if want_v:
        v_ref[...] = v
    if chunked:
        n_g, cl = ws_ref.shape[0], ws_ref.shape[1]
        gd = aw // n_g
        tri = (lax.broadcasted_iota(jnp.int32, (cl, cl), 0) >= lax.broadcasted_iota(jnp.int32, (cl, cl), 1))
        for g in range(n_g):
            wm = jnp.where(tri, ws_ref[g], 0.0).astype(BF16)
            for c in range(tm // cl):
                vg = v[c * cl:(c + 1) * cl, g * gd:(g + 1) * gd].astype(BF16)
                s = jnp.dot(wm, vg, preferred_element_type=F32) + bs_ref[g]
                us_ref[c * cl:(c + 1) * cl, g * gd:(g + 1) * gd] = u[c * cl:(c + 1) * cl, g * gd:(g + 1) * gd] * s
        us = us_ref[...]
    else:
        us = u * (v * ws_ref[...] + bs_ref[...])
    x1 = x + _mm(us, wout_ref[...], precise)
    x1_ref[...] = x1
    _router_tail(x1, gffn_ref[...], wr_ref[...], br_ref[...], precise, runcnt_ref,
                 xn_ref, tope_ref, gate_ref, rank_ref, cnt_ref)


def _full(shape):
    nd = len(shape)
    return pl.BlockSpec(shape, lambda *_: (0,) * nd, pipeline_mode=pl.Buffered(1))


def _router_out_specs(t, tm, d, n_e):
    shapes = [jax.ShapeDtypeStruct((t, TOP_K), jnp.int32), jax.ShapeDtypeStruct((t, TOP_K), F32),
              jax.ShapeDtypeStruct((t, TOP_K), jnp.int32), jax.ShapeDtypeStruct((1, n_e), jnp.int32)]
    specs = [pl.BlockSpec((tm, TOP_K), lambda i: (i, 0)), pl.BlockSpec((tm, TOP_K), lambda i: (i, 0)),
             pl.BlockSpec((tm, TOP_K), lambda i: (i, 0)), pl.BlockSpec((1, n_e), lambda i: (0, 0))]
    return shapes, specs


def _mixer_a(x, p, l, *, chunked, precise, want_v, tm):
    t, d = x.shape
    wdt = F32 if precise else BF16
    w_in = p['a_w_in'][l].astype(wdt)
    w_out = p['a_w_out'][l].astype(wdt)
    aw = w_out.shape[0]
    n_e = p['moe_w_router'].shape[-1]
    if chunked:
        ws = p['a_w_s'][l]
        bs = p['a_b_s'][l][:, :, None]
    else:
        gd = aw // A_GROUPS
        ws = jnp.repeat(p['a_w_s'][l][:, 0, 0], gd)[None, :]
        bs = jnp.repeat(p['a_b_s'][l][:, 0], gd)[None, :]
    args = [x, p['g_mix'][l][None, :], w_in, p['a_b_in'][l][None, :], p['a_g_v'][l][None, :],
            p['a_b_v'][l][None, :], ws, bs, w_out, p['g_ffn'][l][None, :],
            p['moe_w_router'][l].astype(wdt), p['moe_b_router'][l][None, :]]
    in_specs = [pl.BlockSpec((tm, d), lambda i: (i, 0))] + [_full(a.shape) for a in args[1:]]
    r_shapes, r_specs = _router_out_specs(t, tm, d, n_e)
    out_shape = [jax.ShapeDtypeStruct((t, d), F32), jax.ShapeDtypeStruct((t, d), F32)]
    out_specs = [pl.BlockSpec((tm, d), lambda i: (i, 0)), pl.BlockSpec((tm, d), lambda i: (i, 0))]
    if want_v:
        out_shape.append(jax.ShapeDtypeStruct((t, aw), F32))
        out_specs.append(pl.BlockSpec((tm, aw), lambda i: (i, 0)))
    out_shape += r_shapes
    out_specs += r_specs
    scratch = [pltpu.VMEM((1, n_e), F32)]
    if chunked:
        scratch.append(pltpu.VMEM((tm, aw), F32))
    outs = pl.pallas_call(
        functools.partial(_mixer_a_kernel, chunked=chunked, precise=precise, want_v=want_v),
        grid=(t // tm,), in_specs=in_specs, out_specs=out_specs, out_shape=out_shape,
        scratch_shapes=scratch, compiler_params=_cparams(1), name="mixer_a_router",
    )(*args)
    outs = list(outs)
    x1, xn = outs[0], outs[1]
    v = outs[2] if want_v else None
    tope, gates, rank, cnt = outs[-4:]
    return x1, xn, v, tope, gates, rank, cnt


def _route_plan(tope, rank, cnt, tb):
    t = tope.shape[0]
    n_e = cnt.shape[1]
    counts = cnt[0]
    pc = (counts + tb - 1) // tb * tb
    pend = jnp.cumsum(pc)
    pstart = pend - pc
    dest = pstart[tope] + rank
    n_blocks = -(-(t * TOP_K) // tb) + n_e
    blk_e = jnp.minimum(jnp.searchsorted(pend, jnp.arange(n_blocks, dtype=jnp.int32) * tb, side='right'),
                        n_e - 1).astype(jnp.int32)
    n_used = (pend[-1] // tb).astype(jnp.int32)[None]
    return dest.astype(jnp.int32), blk_e, n_used, n_blocks


def _dispatch_kernel(dest_ref, xn_hbm, xs_in_hbm, xs_hbm, sem, *, td):
    del xs_in_hbm
    base = pl.program_id(0) * td

    def issue(t, carry):
        for k in range(TOP_K):
            d = dest_ref[0, 0, t * TOP_K + k]
            pltpu.make_async_copy(xn_hbm.at[pl.ds(base + t, 1)], xs_hbm.at[pl.ds(d, 1)], sem).start()
        return carry
    lax.fori_loop(0, td, issue, 0)
    for _ in range(TOP_K):
        pltpu.make_async_copy(xn_hbm.at[pl.ds(0, td)], xs_hbm.at[pl.ds(0, td)], sem).wait()


def _dispatch(xn, dest, n_rows, td):
    t, d = xn.shape
    dest3 = dest.reshape(t // td, 1, td * TOP_K)
    return pl.pallas_call(
        functools.partial(_dispatch_kernel, td=td),
        grid=(t // td,),
        in_specs=[pl.BlockSpec((1, 1, td * TOP_K), lambda i: (i, 0, 0), memory_space=pltpu.SMEM),
                  pl.BlockSpec(memory_space=pl.ANY), pl.BlockSpec(memory_space=pl.ANY)],
        out_specs=pl.BlockSpec(memory_space=pl.ANY),
        out_shape=jax.ShapeDtypeStruct((n_rows, d), F32),
        scratch_shapes=[pltpu.SemaphoreType.DMA(())],
        input_output_aliases={2: 0},
        compiler_params=_cparams(1), name="moe_dispatch",
    )(dest3, xn, jnp.zeros((n_rows, d), F32))


def _experts_kernel(be_ref, nu_ref, xs_ref, wgu_ref, bgu_ref, wdn_ref, bdn_ref, y_ref, *scratch, precise):
    i = pl.program_id(0)
    live = i < nu_ref[0]
    f = wdn_ref.shape[1]
    if not precise:
        wgu_bf, wdn_bf = scratch
        changed = jnp.logical_or(i == 0, be_ref[i] != be_ref[jnp.maximum(i - 1, 0)])

        @pl.when(jnp.logical_and(live, changed))
        def _():
            wgu_bf[...] = wgu_ref[0].astype(BF16)
            wdn_bf[...] = wdn_ref[0].astype(BF16)

    @pl.when(live)
    def _():
        x = xs_ref[...]
        if precise:
            hgu = _mm(x, wgu_ref[0], True) + bgu_ref[0]
        else:
            hgu = jnp.dot(x.astype(BF16), wgu_bf[...], preferred_element_type=F32) + bgu_ref[0]
        gate = jnp.minimum(hgu[:, :f], SWIGLU_LIMIT)
        up = jnp.clip(hgu[:, f:], -SWIGLU_LIMIT, SWIGLU_LIMIT)
        act = gate * jax.nn.sigmoid(SWIGLU_ALPHA * gate) * (up + 1.0)
        if precise:
            y = _mm(act, wdn_ref[0], True)
        else:
            y = jnp.dot(act.astype(BF16), wdn_bf[...], preferred_element_type=F32)
        y_ref[...] = y + bdn_ref[0]

    @pl.when(jnp.logical_not(live))
    def _():
        y_ref[...] = jnp.zeros_like(y_ref)


def _experts(xs, blk_e, n_used, n_blocks, p, l, tb, precise):
    n_rows, d = xs.shape
    wgu, wdn = p['moe_w_gu'][l], p['moe_w_down'][l]
    n_e, _, f2 = wgu.shape
    f = f2 // 2
    bgu = p['moe_b_gu'][l][:, None, :]
    bdn = p['moe_b_down'][l][:, None, :]

    def row_map(i, be, nu):
        return (jnp.minimum(i, nu[0] - 1), 0)

    def w_map(i, be, nu):
        return (be[jnp.minimum(i, nu[0] - 1)], 0, 0)

    scratch = [] if precise else [pltpu.VMEM((d, f2), BF16), pltpu.VMEM((f, d), BF16)]
    return pl.pallas_call(
        functools.partial(_experts_kernel, precise=precise),
        grid_spec=pltpu.PrefetchScalarGridSpec(
            num_scalar_prefetch=2, grid=(n_blocks,),
            in_specs=[pl.BlockSpec((tb, d), row_map), pl.BlockSpec((1, d, f2), w_map),
                      pl.BlockSpec((1, 1, f2), w_map), pl.BlockSpec((1, f, d), w_map),
                      pl.BlockSpec((1, 1, d), w_map)],
            out_specs=pl.BlockSpec((tb, d), lambda i, be, nu: (i, 0)),
            scratch_shapes=scratch),
        out_shape=jax.ShapeDtypeStruct((n_rows, d), F32),
        compiler_params=_cparams(1), name="moe_experts",
    )(blk_e, n_used, xs, wgu, bgu, wdn, bdn)


def _moe_rows(xn, tope, rank, cnt, p, l, tb, td, precise):
    dest, blk_e, n_used, n_blocks = _route_plan(tope, rank, cnt, tb)
    xs = _dispatch(xn, dest, n_blocks * tb, td)
    y_rows = _experts(xs, blk_e, n_used, n_blocks, p, l, tb, precise)
    return y_rows, dest


def _combine_kvq_kernel(dest_ref, x1_ref, gate_ref, y_hbm, gkv_ref, wk_ref, wv_ref, gmix_ref, wq_ref,
                        x2_ref, k_ref, v_ref, *rest, precise, for_attn):
    if for_attn:
        qb_ref, kb_ref, vt_ref, ybuf, sem = rest
    else:
        q_ref, ybuf, sem = rest
    tm = x1_ref.shape[0]
    x2 = x1_ref[...] + _gather_combine(dest_ref, y_hbm, ybuf, sem, gate_ref[...], tm)
    x2_ref[...] = x2
    hk = _rms(x2, gkv_ref[...])
    k = _mm(hk, wk_ref[...], precise)
    v = _mm(hk, wv_ref[...], precise)
    k_ref[...] = k
    v_ref[...] = v
    q = _mm(_rms(x2, gmix_ref[...]), wq_ref[...], precise)
    if for_attn:
        qb_ref[...] = (q * ATTN_SCALE).astype(BF16)
        kb_ref[...] = k.astype(BF16)
        vt_ref[:, 0] = v.T.reshape(vt_ref.shape[0], vt_ref.shape[2], tm).astype(BF16)
    else:
        q_ref[...] = q


def _combine_kvq(x1, gates, y_rows, dest, p, *, precise, for_attn, tm):
    t, d = x1.shape
    wdt = F32 if precise else BF16
    dest3 = dest.reshape(t // tm, 1, tm * TOP_K)
    args = [dest3, x1, gates, y_rows, p['g_kv'][None, :], p['w_k'].astype(wdt), p['w_v'].astype(wdt),
            p['g_mix'][N_A_LAYERS][None, :], p['b_w_q'][0].astype(wdt)]
    tile = pl.BlockSpec((tm, d), lambda i: (i, 0))
    in_specs = [pl.BlockSpec((1, 1, tm * TOP_K), lambda i: (i, 0, 0), memory_space=pltpu.SMEM),
                tile, pl.BlockSpec((tm, TOP_K), lambda i: (i, 0)), pl.BlockSpec(memory_space=pl.ANY)]
    in_specs += [_full(a.shape) for a in args[4:]]
    out_shape = [jax.ShapeDtypeStruct((t, d), F32)] * 3
    out_specs = [tile] * 3
    if for_attn:
        n_h = d // V7X_LANES
        out_shape += [jax.ShapeDtypeStruct((t, d), BF16), jax.ShapeDtypeStruct((t, d), BF16),
                      jax.ShapeDtypeStruct((n_h, t // tm, V7X_LANES, tm), BF16)]
        out_specs += [tile, tile, pl.BlockSpec((n_h, 1, V7X_LANES, tm), lambda i: (0, i, 0, 0))]
    else:
        out_shape += [jax.ShapeDtypeStruct((t, d), F32)]
        out_specs += [tile]
    return pl.pallas_call(
        functools.partial(_combine_kvq_kernel, precise=precise, for_attn=for_attn),
        grid=(t // tm,), in_specs=in_specs, out_specs=out_specs, out_shape=out_shape,
        scratch_shapes=[pltpu.VMEM((TOP_K, tm, d), F32), pltpu.SemaphoreType.DMA(())],
        compiler_params=_cparams(1), name="moe_combine_kvq",
    )(*args)


def _bucket_np(n):
    n = np.asarray(n)
    nf = np.maximum(n, 1).astype(np.float32)
    large = MAX_EXACT + (np.log(nf / np.float32(MAX_EXACT)) / np.float32(math.log(MAX_DISTANCE / MAX_EXACT))
                         * np.float32(N_BUCKETS - MAX_EXACT)).astype(np.int32)
    return np.where(n < MAX_EXACT, n, np.minimum(large, N_BUCKETS - 1)).astype(np.int32)


def _bias_tiles_kernel(tbl_ref, bkt_ref, bkt_s_ref, head_s_ref, tile_ref, last_ref, zero_ref):
    h = pl.program_id(0)
    far = tbl_ref[N_BUCKETS - 1, h]
    for dl in range(2):
        b = bkt_ref[dl]
        acc = jnp.where(b < 0, NEG_BIG, 0.0).astype(F32)
        for bb in range(N_BUCKETS - 1):
            acc = jnp.where(b == bb, tbl_ref[bb, h] - far, acc)
        tile_ref[0, dl] = acc
    @pl.when(h == 0)
    def _():
        last_ref[...] = jnp.zeros_like(last_ref)
        zero_ref[...] = jnp.zeros_like(zero_ref)

    bs = bkt_s_ref[...]
    sel = head_s_ref[...] == h
    acc = last_ref[...]
    for bb in range(N_BUCKETS - 1):
        acc = jnp.where(jnp.logical_and(sel, bs == bb), tbl_ref[bb, h] - far, acc)
    last_ref[...] = acc
    row = lax.broadcasted_iota(jnp.int32, zero_ref.shape, 0) % N_HEADS
    zero_ref[...] = jnp.where(row == h, tbl_ref[0, h] - far, zero_ref[...])


def _bias_tiles(rel_table, tb, page):
    assert tb >= MAX_DISTANCE and page >= MAX_DISTANCE
    kq = np.arange(tb)
    tiles = []
    for dl in range(2):
        dist = dl * tb + kq[None, :] - kq[:, None]
        tiles.append(np.where(dist >= 0, _bucket_np(np.maximum(dist, 0)), -1))
    bkt = jnp.asarray(np.stack(tiles).astype(np.int32))
    lanes = np.arange(page * N_HEADS)
    bkt_s = jnp.asarray(_bucket_np(page - lanes // N_HEADS)[None, :].astype(np.int32))
    head_s = jnp.asarray((lanes % N_HEADS)[None, :].astype(np.int32))
    return pl.pallas_call(
        _bias_tiles_kernel,
        grid=(N_HEADS,),
        in_specs=[pl.BlockSpec(memory_space=pltpu.SMEM), _full(bkt.shape), _full(bkt_s.shape),
                  _full(head_s.shape)],
        out_specs=[pl.BlockSpec((1, 2, tb, tb), lambda h: (h, 0, 0, 0)),
                   pl.BlockSpec(bkt_s.shape, lambda h: (0, 0)),
                   pl.BlockSpec((2 * N_HEADS, V7X_LANES), lambda h: (0, 0))],
        out_shape=[jax.ShapeDtypeStruct((N_HEADS, 2, tb, tb), F32),
                   jax.ShapeDtypeStruct(bkt_s.shape, F32),
                   jax.ShapeDtypeStruct((2 * N_HEADS, V7X_LANES), F32)],
        compiler_params=_cparams(1), name="rel_bias_tiles",
    )(rel_table, bkt, bkt_s, head_s)


def _attn_prompt_kernel(lam_ref, q_ref, k_ref, vt_ref, bias_ref, o_ref, acc_ref, m_ref, l_ref, *, tb):
    qi = pl.program_id(1)
    q = q_ref[...]
    lane = lax.broadcasted_iota(jnp.int32, q.shape, 1)
    zero = jnp.zeros_like(q)
    qs = (jnp.where(lane < HEAD_DIM, q, zero), jnp.where(lane >= HEAD_DIM, q, zero))
    m_ref[...] = jnp.full_like(m_ref, NEG_BIG)
    l_ref[...] = jnp.zeros_like(l_ref)
    acc_ref[...] = jnp.zeros_like(acc_ref)

    def step(j, bias):
        k_blk = k_ref[pl.ds(pl.multiple_of(j * tb, tb), tb), :]
        vt = vt_ref[0, j]
        for mp in range(2):
            s = lax.dot_general(k_blk, qs[mp], (((1,), (1,)), ((), ())), preferred_element_type=F32)
            if bias is not None:
                s = s + bias
            m_old = m_ref[mp]
            m_new = jnp.maximum(m_old, jnp.max(s, axis=0, keepdims=True))
            alpha = jnp.exp(m_old - m_new)
            pr = jnp.exp(s - m_new)
            l_ref[mp] = alpha * l_ref[mp] + jnp.sum(pr, axis=0, keepdims=True)
            acc_ref[mp] = alpha * acc_ref[mp] + jnp.dot(vt, pr.astype(BF16), preferred_element_type=F32)
            m_ref[mp] = m_new

    def far_step(j, carry):
        step(j, None)
        return carry
    lax.fori_loop(0, jnp.maximum(qi - 1, 0), far_step, 0)

    @pl.when(qi >= 1)
    def _():
        step(qi - 1, bias_ref[0, 1])
    step(qi, bias_ref[0, 0])
    o_t = acc_ref[0] / l_ref[0] - lam_ref[0] * (acc_ref[1] / l_ref[1])
    o_ref[...] = o_t.T


def _attn_prompt(qb, kb, vt4, bias_tiles, lam, tb):
    t, d = qb.shape
    n_h = d // V7X_LANES
    n_q = t // tb
    return pl.pallas_call(
        functools.partial(_attn_prompt_kernel, tb=tb),
        grid=(n_h, n_q),
        in_specs=[pl.BlockSpec(memory_space=pltpu.SMEM),
                  pl.BlockSpec((tb, V7X_LANES), lambda h, i: (i, h)),
                  pl.BlockSpec((t, V7X_LANES), lambda h, i: (0, h)),
                  pl.BlockSpec((1, n_q, V7X_LANES, tb), lambda h, i: (h, 0, 0, 0)),
                  pl.BlockSpec((1, 2, tb, tb), lambda h, i: (h, 0, 0, 0))],
        out_specs=pl.BlockSpec((tb, V7X_LANES), lambda h, i: (i, h)),
        out_shape=jax.ShapeDtypeStruct((t, d), F32),
        scratch_shapes=[pltpu.VMEM((2, V7X_LANES, tb), F32), pltpu.VMEM((2, 1, tb), F32),
                        pltpu.VMEM((2, 1, tb), F32)],
        compiler_params=_cparams(2), name="attn_prompt",
    )(lam, qb, kb, vt4, bias_tiles)


def _attn_sample_kernel(pt_ref, lam_ref, q_ref, kn_ref, vn_ref, mask_ref, blast_ref, bzero_ref, *rest, n_pp):
    k_refs = rest[:n_pp]
    v_refs = rest[n_pp:2 * n_pp]
    o_ref, acc_ref, m_ref, l_ref = rest[2 * n_pp:]
    del pt_ref
    g = pl.program_id(1)
    n_g = pl.num_programs(1)
    n_h = N_HEADS
    q = q_ref[0].astype(BF16)

    @pl.when(g == 0)
    def _():
        s0 = jnp.sum(q.astype(F32) * kn_ref[0].astype(BF16).astype(F32), axis=-1, keepdims=True)
        m_ref[...] = s0 + bzero_ref[:, 0:1]
        l_ref[...] = jnp.ones_like(l_ref)
        vn = vn_ref[0].astype(BF16).astype(F32)
        acc_ref[...] = jnp.concatenate([vn, vn], axis=0)

    keys = v_refs[0].shape[1] // n_h
    for i in range(n_pp):
        k_even = k_refs[i][0, pl.ds(0, keys * n_h, stride=2), :].astype(BF16)
        k_odd = k_refs[i][0, pl.ds(1, keys * n_h, stride=2), :].astype(BF16)
        dn = (((1,), (1,)), ((), ()))
        s = jnp.concatenate([lax.dot_general(q[:n_h], k_even, dn, preferred_element_type=F32),
                             lax.dot_general(q[n_h:], k_odd, dn, preferred_element_type=F32)], axis=0)
        s = s + mask_ref[...]
        if i == n_pp - 1:
            s = s + jnp.where(g == n_g - 1, 1.0, 0.0) * blast_ref[...]
        m_old = m_ref[...]
        m_new = jnp.maximum(m_old, jnp.max(s, axis=-1, keepdims=True))
        alpha = jnp.exp(m_old - m_new)
        pr = jnp.exp(s - m_new)
        l_ref[...] = alpha * l_ref[...] + jnp.sum(pr, axis=-1, keepdims=True)
        acc_ref[...] = alpha * acc_ref[...] + jnp.dot(pr.astype(BF16), v_refs[i][0].astype(BF16),
                                                      preferred_element_type=F32)
        m_ref[...] = m_new

    @pl.when(g == n_g - 1)
    def _():
        o = acc_ref[...] / l_ref[...]
        o_ref[0] = o[:n_h] - lam_ref[0] * o[n_h:]


def _attn_sample(q, k_new, v_new, cache_k, cache_v, page_table, blast, bzero, lam):
    nb, d = q.shape
    n_phys, page = cache_k.shape[0], cache_k.shape[1]
    n_pages = page_table.shape[1]
    n_pp = math.gcd(PAGES_PER_STEP, n_pages)
    n_h = N_HEADS
    perm = np.concatenate([np.arange(0, 2 * n_h, 2), np.arange(1, 2 * n_h, 2)])
    qp = (q * ATTN_SCALE).reshape(nb, 2 * n_h, HEAD_DIM)[:, perm]
    kn = k_new.reshape(nb, 2 * n_h, HEAD_DIM)[:, perm]
    vn = v_new.reshape(nb, n_h, V_DIM)
    ck = cache_k.reshape(n_phys, page * 2 * n_h, HEAD_DIM)
    cv = cache_v.reshape(n_phys, page * n_h, V_DIM)
    lanes = np.arange(page * n_h)
    rows = np.arange(2 * n_h)
    mask = jnp.asarray(np.where(lanes[None, :] % n_h == rows[:, None] % n_h, 0.0, NEG_BIG).astype(np.float32))

    def page_map(i):
        return lambda b, g, pt: (pt[b * n_pages + g * n_pp + i], 0, 0)
    row3 = lambda b, g, pt: (b, 0, 0)
    const2 = lambda b, g, pt: (0, 0)
    in_specs = [pl.BlockSpec(memory_space=pltpu.SMEM),
                pl.BlockSpec((1, 2 * n_h, HEAD_DIM), row3), pl.BlockSpec((1, 2 * n_h, HEAD_DIM), row3),
                pl.BlockSpec((1, n_h, V_DIM), row3), pl.BlockSpec(mask.shape, const2),
                pl.BlockSpec(blast.shape, const2), pl.BlockSpec(bzero.shape, const2)]
    in_specs += [pl.BlockSpec((1, page * 2 * n_h, HEAD_DIM), page_map(i)) for i in range(n_pp)]
    in_specs += [pl.BlockSpec((1, page * n_h, V_DIM), page_map(i)) for i in range(n_pp)]
    out = pl.pallas_call(
        functools.partial(_attn_sample_kernel, n_pp=n_pp),
        grid_spec=pltpu.PrefetchScalarGridSpec(
            num_scalar_prefetch=1, grid=(nb, n_pages // n_pp), in_specs=in_specs,
            out_specs=pl.BlockSpec((1, n_h, V_DIM), row3),
            scratch_shapes=[pltpu.VMEM((2 * n_h, V_DIM), F32), pltpu.VMEM((2 * n_h, 1), F32),
                            pltpu.VMEM((2 * n_h, 1), F32)]),
        out_shape=jax.ShapeDtypeStruct((nb, n_h, V_DIM), F32),
        compiler_params=_cparams(2), name="attn_sample",
    )(page_table.reshape(-1).astype(jnp.int32), lam, qp, kn, vn, mask, blast, bzero,
      *([ck] * n_pp), *([cv] * n_pp))
    return out.reshape(nb, d)


def _attn_out_kernel(o_ref, x2_ref, gsub_ref, wo_ref, gffn_ref, wr_ref, br_ref,
                     x3_ref, xn_ref, tope_ref, gate_ref, rank_ref, cnt_ref, runcnt_ref, on_ref,
                     *, precise, sub_scale):
    @pl.when(pl.program_id(0) == 0)
    def _():
        runcnt_ref[...] = jnp.zeros_like(runcnt_ref)

    vd = gsub_ref.shape[1]
    for h in range(o_ref.shape[1] // vd):
        oh = o_ref[:, h * vd:(h + 1) * vd]
        on_ref[:, h * vd:(h + 1) * vd] = _rms(oh, gsub_ref[...]) * sub_scale
    x3 = x2_ref[...] + _mm(on_ref[...], wo_ref[...], precise)
    x3_ref[...] = x3
    _router_tail(x3, gffn_ref[...], wr_ref[...], br_ref[...], precise, runcnt_ref,
                 xn_ref, tope_ref, gate_ref, rank_ref, cnt_ref)


def _attn_out(o, x2, p, l, sub_scale, *, precise, tm):
    t, d = x2.shape
    wdt = F32 if precise else BF16
    n_e = p['moe_w_router'].shape[-1]
    args = [o, x2, p['b_g_sub'][0][None, :], p['b_w_o'][0].astype(wdt), p['g_ffn'][l][None, :],
            p['moe_w_router'][l].astype(wdt), p['moe_b_router'][l][None, :]]
    tile = pl.BlockSpec((tm, d), lambda i: (i, 0))
    r_shapes, r_specs = _router_out_specs(t, tm, d, n_e)
    outs = pl.pallas_call(
        functools.partial(_attn_out_kernel, precise=precise, sub_scale=sub_scale),
        grid=(t // tm,), in_specs=[tile, tile] + [_full(a.shape) for a in args[2:]],
        out_specs=[tile, tile] + r_specs,
        out_shape=[jax.ShapeDtypeStruct((t, d), F32)] * 2 + r_shapes,
        scratch_shapes=[pltpu.VMEM((1, n_e), F32), pltpu.VMEM((tm, d), F32)],
        compiler_params=_cparams(1), name="attn_out_router",
    )(*args)
    return outs


def _combine_final_kernel(dest_ref, x3_ref, gate_ref, y_hbm, gout_ref, y_ref, ybuf, sem):
    tm = x3_ref.shape[0]
    x4 = x3_ref[...] + _gather_combine(dest_ref, y_hbm, ybuf, sem, gate_ref[...], tm)
    y_ref[...] = _rms(x4, gout_ref[...])


def _combine_final(x3, gates, y_rows, dest, g_out, tm):
    t, d = x3.shape
    dest3 = dest.reshape(t // tm, 1, tm * TOP_K)
    tile = pl.BlockSpec((tm, d), lambda i: (i, 0))
    return pl.pallas_call(
        _combine_final_kernel,
        grid=(t // tm,),
        in_specs=[pl.BlockSpec((1, 1, tm * TOP_K), lambda i: (i, 0, 0), memory_space=pltpu.SMEM),
                  tile, pl.BlockSpec((tm, TOP_K), lambda i: (i, 0)), pl.BlockSpec(memory_space=pl.ANY),
                  _full((1, d))],
        out_specs=tile, out_shape=jax.ShapeDtypeStruct((t, d), F32),
        scratch_shapes=[pltpu.VMEM((TOP_K, tm, d), F32), pltpu.SemaphoreType.DMA(())],
        compiler_params=_cparams(1), name="moe_combine_final",
    )(dest3, x3, gates, y_rows, g_out[None, :])


def _trunk(x, p, lam, sub_scale, attend, *, sample):
    t, d = x.shape
    tm = t if sample else PROMPT_TOKEN_TILE
    tb = SAMPLE_EXPERT_BLOCK if sample else PROMPT_EXPERT_BLOCK
    assert t % tm == 0 and (sample or tm % CHUNK == 0)
    x1, xn, v_rows, tope, gates, rank, cnt = _mixer_a(
        x, p, 0, chunked=not sample, precise=sample, want_v=sample, tm=tm)
    y_rows, dest = _moe_rows(xn, tope, rank, cnt, p, 0, tb, tm, sample)
    outs = _combine_kvq(x1, gates, y_rows, dest, p, precise=sample, for_attn=not sample, tm=tm)
    x2, k, v = outs[:3]
    o = attend(*outs[3:], k, v)
    x3, xn, tope, gates, rank, cnt = _attn_out(o, x2, p, 1, sub_scale, precise=sample, tm=tm)
    y_rows, dest = _moe_rows(xn, tope, rank, cnt, p, 1, tb, tm, sample)
    y = _combine_final(x3, gates, y_rows, dest, p['g_out'], tm)
    return y, k, v, v_rows


def kernel(x_prompt, x_sample, cache_k, cache_v, page_table, g_mix, g_ffn, g_kv, g_out, a_w_in, a_b_in, a_g_v, a_b_v, a_w_s, a_b_s, a_w_out, w_k, w_v, b_w_q, b_lq1, b_lk1, b_lq2, b_lk2, b_g_sub, b_w_o, rel_table, moe_w_router, moe_b_router, moe_w_gu, moe_b_gu, moe_w_down, moe_b_down):
    p = {
        'g_mix': g_mix, 'g_ffn': g_ffn, 'g_kv': g_kv, 'g_out': g_out,
        'a_w_in': a_w_in, 'a_b_in': a_b_in, 'a_g_v': a_g_v, 'a_b_v': a_b_v,
        'a_w_s': a_w_s, 'a_b_s': a_b_s, 'a_w_out': a_w_out,
        'w_k': w_k, 'w_v': w_v, 'b_w_q': b_w_q, 'b_g_sub': b_g_sub, 'b_w_o': b_w_o,
        'moe_w_router': moe_w_router, 'moe_b_router': moe_b_router,
        'moe_w_gu': moe_w_gu, 'moe_b_gu': moe_b_gu, 'moe_w_down': moe_w_down, 'moe_b_down': moe_b_down,
    }
    assert x_prompt.shape[0] == 1 and x_sample.shape[1] == 1
    b, s, d = x_prompt.shape
    nb = x_sample.shape[0]
    page = cache_k.shape[1]
    lam_init = 0.8 - 0.6 * math.exp(-0.3 * N_A_LAYERS)
    lam = (jnp.exp(jnp.sum(b_lq1[0] * b_lk1[0])) - jnp.exp(jnp.sum(b_lq2[0] * b_lk2[0])) + lam_init)
    lam = lam.astype(F32)[None]
    tb = min(ATTN_BLOCK, s)
    bias_tiles, blast, bzero = _bias_tiles(rel_table, tb, page)

    def attend_prompt(qb, kb, vt4, k, v):
        return _attn_prompt(qb, kb, vt4, bias_tiles, lam, tb)

    def attend_sample(q, k, v):
        return _attn_sample(q, k, v, cache_k, cache_v, page_table, blast, bzero, lam)

    y_p, k_p, v_p, _ = _trunk(x_prompt[0], p, lam, 1.0 - lam_init, attend_prompt, sample=False)
    y_s, k_s, v_s, cv_s = _trunk(x_sample[:, 0], p, lam, 1.0 - lam_init, attend_sample, sample=True)
    n_h = N_HEADS
    return (y_p[None], y_s[:, None],
            k_p.reshape(1, s, 2 * n_h, HEAD_DIM), v_p.reshape(1, s, n_h, V_DIM),
            k_s.reshape(nb, 1, 2 * n_h, HEAD_DIM), v_s.reshape(nb, 1, n_h, V_DIM),
            cv_s.reshape(1, nb, 1, -1))
```

```python
import functools
import math

import numpy as np
import jax
import jax.numpy as jnp
from jax import lax
from jax.experimental import pallas as pl
from jax.experimental.pallas import tpu as pltpu

F32 = jnp.float32
BF16 = jnp.bfloat16

EPS = 1e-6
CHUNK = 128
A_GROUPS = 8
N_HEADS = 8
HEAD_DIM = 64
V_DIM = 2 * HEAD_DIM
ATTN_SCALE = HEAD_DIM ** -0.5
N_BUCKETS = 32
MAX_EXACT = N_BUCKETS // 2
MAX_DISTANCE = 128
TOP_K = 4
SWIGLU_LIMIT = 7.0
SWIGLU_ALPHA = 1.702
N_A_LAYERS = 1

V7X_LANES = 128
V7X_SUBLANES = 8
V7X_VMEM_LIMIT_BYTES = 56 * 1024 * 1024

NEG_BIG = -1e30
PROMPT_TOKEN_TILE = 256
PROMPT_EXPERT_BLOCK = 256
ATTN_BLOCK = 512
LOG2E = math.log2(math.e)
SAMPLE_EXPERT_BLOCK = 8
PAGES_PER_STEP = 4


def _cparams(n_axes):
    return pltpu.CompilerParams(
        dimension_semantics=("arbitrary",) * n_axes,
        vmem_limit_bytes=V7X_VMEM_LIMIT_BYTES,
    )


def _mm(a, w):
    return jnp.dot(a.astype(BF16), w.astype(BF16), preferred_element_type=F32)


def _rms(x, g):
    return x * lax.rsqrt(jnp.mean(x * x, axis=-1, keepdims=True) + EPS) * g


def _gelu(x):
    return 0.5 * x * (1.0 + lax.erf(x * (2.0 ** -0.5)))


def _cols4(c0, c1, c2, c3):
    m = c0.shape[0]
    lane = lax.broadcasted_iota(jnp.int32, (m, TOP_K), 1)
    return jnp.where(lane == 0, c0, jnp.where(lane == 1, c1, jnp.where(lane == 2, c2, c3)))


def _router_tail(x1, gffn, wr, br, runcnt_ref, xn_ref, tope_ref, gate_ref, rank_ref, cnt_ref):
    m = x1.shape[0]
    xn = _rms(x1, gffn)
    xn_ref[...] = xn
    logits = _mm(xn, wr) + br
    n_e = logits.shape[1]
    lane = lax.broadcasted_iota(jnp.int32, (m, n_e), 1).astype(F32)
    vals, sels, ohs = [], [], []
    cur = logits
    for _ in range(TOP_K):
        mx = jnp.max(cur, axis=-1, keepdims=True)
        sel = jnp.min(jnp.where(cur == mx, lane, float(n_e)), axis=-1, keepdims=True)
        oh = lane == sel
        vals.append(mx)
        sels.append(sel)
        ohs.append(oh.astype(F32))
        cur = jnp.where(oh, -jnp.inf, cur)
    es = [jnp.exp(v - vals[0]) for v in vals]
    den = es[0] + es[1] + es[2] + es[3]
    gate_ref[...] = _cols4(*[e / den for e in es])
    tope_ref[...] = _cols4(*sels).astype(jnp.int32)
    row = lax.broadcasted_iota(jnp.int32, (m, m), 0)
    col = lax.broadcasted_iota(jnp.int32, (m, m), 1)
    ltri = (row > col).astype(BF16)
    prior = runcnt_ref[...]
    ranks = []
    for oh in ohs:
        within = jnp.dot(ltri, oh.astype(BF16), preferred_element_type=F32)
        ranks.append(jnp.sum((prior + within) * oh, axis=-1, keepdims=True))
        prior = prior + jnp.sum(oh, axis=0, keepdims=True)
    rank_ref[...] = _cols4(*ranks).astype(jnp.int32)
    runcnt_ref[...] = prior
    cnt_ref[...] = prior.astype(jnp.int32)


def _gather_combine(dest_ref, y_hbm, ybuf, sem, gates, tm):
    def issue(t, carry):
        for k in range(TOP_K):
            d = dest_ref[0, 0, t * TOP_K + k]
            pltpu.make_async_copy(y_hbm.at[pl.ds(d, 1)], ybuf.at[k, pl.ds(t, 1)], sem).start()
        return carry
    lax.fori_loop(0, tm, issue, 0)
    for k in range(TOP_K):
        pltpu.make_async_copy(y_hbm.at[pl.ds(0, tm)], ybuf.at[k], sem).wait()
    acc = gates[:, 0:1] * ybuf[0]
    for k in range(1, TOP_K):
        acc = acc + gates[:, k:k + 1] * ybuf[k]
    return acc


def _mixer_a_kernel(*refs, chunked, want_v):
    (x_ref, gmix_ref, win_ref, bin_ref, gv_ref, bv_ref, ws_ref, bs_ref, wout_ref,
     gffn_ref, wr_ref, br_ref) = refs[:12]
    rest = list(refs[12:])
    x1_ref, xn_ref = rest[0], rest[1]
    rest = rest[2:]
    v_ref = rest.pop(0) if want_v else None
    tope_ref, gate_ref, rank_ref, cnt_ref, runcnt_ref = rest[:5]
    us_ref = rest[5] if chunked else None

    @pl.when(pl.program_id(0) == 0)
    def _():
        runcnt_ref[...] = jnp.zeros_like(runcnt_ref)

    x = x_ref[...]
    tm = x.shape[0]
    aw = wout_ref.shape[0]
    h = _rms(x, gmix_ref[...])
    u = _gelu(_mm(h, win_ref[:, :aw]) + bin_ref[:, :aw])
    vr = _gelu(_mm(h, win_ref[:, aw:]) + bin_ref[:, aw:])
    mu = jnp.mean(vr, axis=-1, keepdims=True)
    vc = vr - mu
    v = vc * lax.rsqrt(jnp.mean(vc * vc, axis=-1, keepdims=True) + EPS) * gv_ref[...] + bv_ref[...]
    if want_v:
        v_ref[...] = v
    if chunked:
        n_g, cl = ws_ref.shape[0], ws_ref.shape[1]
        gd = aw // n_g
        tri = (lax.broadcasted_iota(jnp.int32, (cl, cl), 0) >= lax.broadcasted_iota(jnp.int32, (cl, cl), 1))
        for g in range(n_g):
            wm = jnp.where(tri, ws_ref[g], 0.0).astype(BF16)
            for c in range(tm // cl):
                vg = v[c * cl:(c + 1) * cl, g * gd:(g + 1) * gd].astype(BF16)
                s = jnp.dot(wm, vg, preferred_element_type=F32) + bs_ref[g]
                us_ref[c * cl:(c + 1) * cl, g * gd:(g + 1) * gd] = u[c * cl:(c + 1) * cl, g * gd:(g + 1) * gd] * s
        us = us_ref[...]
    else:
        us = u * (v * ws_ref[...] + bs_ref[...])
    x1 = x + _mm(us, wout_ref[...])
    x1_ref[...] = x1
    _router_tail(x1, gffn_ref[...], wr_ref[...], br_ref[...], runcnt_ref,
                 xn_ref, tope_ref, gate_ref, rank_ref, cnt_ref)


def _full(shape):
    nd = len(shape)
    return pl.BlockSpec(shape, lambda *_: (0,) * nd, pipeline_mode=pl.Buffered(1))


def _router_out_specs(t, tm, d, n_e):
    shapes = [jax.ShapeDtypeStruct((t, TOP_K), jnp.int32), jax.ShapeDtypeStruct((t, TOP_K), F32),
              jax.ShapeDtypeStruct((t, TOP_K), jnp.int32), jax.ShapeDtypeStruct((1, n_e), jnp.int32)]
    specs = [pl.BlockSpec((tm, TOP_K), lambda i: (i, 0)), pl.BlockSpec((tm, TOP_K), lambda i: (i, 0)),
             pl.BlockSpec((tm, TOP_K), lambda i: (i, 0)), pl.BlockSpec((1, n_e), lambda i: (0, 0))]
    return shapes, specs


def _mixer_a(x, p, l, *, chunked, want_v, tm):
    t, d = x.shape
    w_in = p['a_w_in'][l].astype(BF16)
    w_out = p['a_w_out'][l].astype(BF16)
    aw = w_out.shape[0]
    n_e = p['moe_w_router'].shape[-1]
    if chunked:
        ws = p['a_w_s'][l]
        bs = p['a_b_s'][l][:, :, None]
    else:
        gd = aw // A_GROUPS
        ws = jnp.repeat(p['a_w_s'][l][:, 0, 0], gd)[None, :]
        bs = jnp.repeat(p['a_b_s'][l][:, 0], gd)[None, :]
    args = [x, p['g_mix'][l][None, :], w_in, p['a_b_in'][l][None, :], p['a_g_v'][l][None, :],
            p['a_b_v'][l][None, :], ws, bs, w_out, p['g_ffn'][l][None, :],
            p['moe_w_router'][l].astype(BF16), p['moe_b_router'][l][None, :]]
    in_specs = [pl.BlockSpec((tm, d), lambda i: (i, 0))] + [_full(a.shape) for a in args[1:]]
    r_shapes, r_specs = _router_out_specs(t, tm, d, n_e)
    out_shape = [jax.ShapeDtypeStruct((t, d), F32), jax.ShapeDtypeStruct((t, d), F32)]
    out_specs = [pl.BlockSpec((tm, d), lambda i: (i, 0)), pl.BlockSpec((tm, d), lambda i: (i, 0))]
    if want_v:
        out_shape.append(jax.ShapeDtypeStruct((t, aw), F32))
        out_specs.append(pl.BlockSpec((tm, aw), lambda i: (i, 0)))
    out_shape += r_shapes
    out_specs += r_specs
    scratch = [pltpu.VMEM((1, n_e), F32)]
    if chunked:
        scratch.append(pltpu.VMEM((tm, aw), F32))
    outs = pl.pallas_call(
        functools.partial(_mixer_a_kernel, chunked=chunked, want_v=want_v),
        grid=(t // tm,), in_specs=in_specs, out_specs=out_specs, out_shape=out_shape,
        scratch_shapes=scratch, compiler_params=_cparams(1), name="mixer_a_router",
    )(*args)
    outs = list(outs)
    x1, xn = outs[0], outs[1]
    v = outs[2] if want_v else None
    tope, gates, rank, cnt = outs[-4:]
    return x1, xn, v, tope, gates, rank, cnt


def _route_plan(tope, rank, cnt, tb):
    t = tope.shape[0]
    n_e = cnt.shape[1]
    counts = cnt[0]
    pc = (counts + tb - 1) // tb * tb
    pend = jnp.cumsum(pc)
    pstart = pend - pc
    dest = pstart[tope] + rank
    n_blocks = -(-(t * TOP_K) // tb) + n_e
    blk_e = jnp.minimum(jnp.searchsorted(pend, jnp.arange(n_blocks, dtype=jnp.int32) * tb, side='right'),
                        n_e - 1).astype(jnp.int32)
    n_used = (pend[-1] // tb).astype(jnp.int32)[None]
    return dest.astype(jnp.int32), blk_e, n_used, n_blocks


def _dispatch_kernel(dest_ref, xn_ref, xs_in_hbm, xs_hbm, sem, *, td):
    del xs_in_hbm

    def issue(t, carry):
        for k in range(TOP_K):
            d = dest_ref[0, 0, t * TOP_K + k]
            pltpu.make_async_copy(xn_ref.at[pl.ds(t, 1)], xs_hbm.at[pl.ds(d, 1)], sem).start()
        return carry
    lax.fori_loop(0, td, issue, 0)
    for _ in range(TOP_K):
        pltpu.make_async_copy(xn_ref, xs_hbm.at[pl.ds(0, td)], sem).wait()


def _dispatch(xn, dest, n_rows, td):
    t, d = xn.shape
    dest3 = dest.reshape(t // td, 1, td * TOP_K)
    return pl.pallas_call(
        functools.partial(_dispatch_kernel, td=td),
        grid=(t // td,),
        in_specs=[pl.BlockSpec((1, 1, td * TOP_K), lambda i: (i, 0, 0), memory_space=pltpu.SMEM),
                  pl.BlockSpec((td, d), lambda i: (i, 0)), pl.BlockSpec(memory_space=pl.ANY)],
        out_specs=pl.BlockSpec(memory_space=pl.ANY),
        out_shape=jax.ShapeDtypeStruct((n_rows, d), F32),
        scratch_shapes=[pltpu.SemaphoreType.DMA(())],
        input_output_aliases={2: 0},
        compiler_params=_cparams(1), name="moe_dispatch",
    )(dest3, xn, jnp.zeros((n_rows, d), F32))


def _experts_kernel(be_ref, nu_ref, xs_ref, wgu_ref, bgu_ref, wdn_ref, bdn_ref, y_ref, wgu_bf, wdn_bf):
    i = pl.program_id(0)
    live = i < nu_ref[0]
    f = wdn_ref.shape[2]
    changed = jnp.logical_or(i == 0, be_ref[i] != be_ref[jnp.maximum(i - 1, 0)])

    @pl.when(jnp.logical_and(live, changed))
    def _():
        wgu_bf[...] = wgu_ref[0, 0].astype(BF16)
        wdn_bf[...] = wdn_ref[0, 0].astype(BF16)

    @pl.when(live)
    def _():
        x = xs_ref[...]
        hgu = _mm(x, wgu_bf[...]) + bgu_ref[0, 0]
        gate = jnp.minimum(hgu[:, :f], SWIGLU_LIMIT)
        up = jnp.clip(hgu[:, f:], -SWIGLU_LIMIT, SWIGLU_LIMIT)
        act = gate * jax.nn.sigmoid(SWIGLU_ALPHA * gate) * (up + 1.0)
        y_ref[...] = _mm(act, wdn_bf[...]) + bdn_ref[0, 0]

    @pl.when(jnp.logical_not(live))
    def _():
        y_ref[...] = jnp.zeros_like(y_ref)


def _experts(xs, blk_e, n_used, n_blocks, p, l, tb):
    n_rows, d = xs.shape
    wgu, wdn = p['moe_w_gu'], p['moe_w_down']
    _, n_e, _, f2 = wgu.shape
    f = f2 // 2
    bgu = p['moe_b_gu'][:, :, None, :]
    bdn = p['moe_b_down'][:, :, None, :]

    def row_map(i, be, nu):
        return (jnp.minimum(i, nu[0] - 1), 0)

    def w_map(i, be, nu):
        return (l, be[jnp.minimum(i, nu[0] - 1)], 0, 0)

    return pl.pallas_call(
        _experts_kernel,
        grid_spec=pltpu.PrefetchScalarGridSpec(
            num_scalar_prefetch=2, grid=(n_blocks,),
            in_specs=[pl.BlockSpec((tb, d), row_map), pl.BlockSpec((1, 1, d, f2), w_map),
                      pl.BlockSpec((1, 1, 1, f2), w_map), pl.BlockSpec((1, 1, f, d), w_map),
                      pl.BlockSpec((1, 1, 1, d), w_map)],
            out_specs=pl.BlockSpec((tb, d), lambda i, be, nu: (i, 0)),
            scratch_shapes=[pltpu.VMEM((d, f2), BF16), pltpu.VMEM((f, d), BF16)]),
        out_shape=jax.ShapeDtypeStruct((n_rows, d), F32),
        compiler_params=_cparams(1), name="moe_experts",
    )(blk_e, n_used, xs, wgu, bgu, wdn, bdn)


def _moe_rows(xn, tope, rank, cnt, p, l, tb, td):
    dest, blk_e, n_used, n_blocks = _route_plan(tope, rank, cnt, tb)
    xs = _dispatch(xn, dest, n_blocks * tb, td)
    y_rows = _experts(xs, blk_e, n_used, n_blocks, p, l, tb)
    return y_rows, dest


def _combine_kvq_kernel(dest_ref, x1_ref, gate_ref, y_hbm, gkv_ref, wk_ref, wv_ref, gmix_ref, wq_ref,
                        x2_ref, k_ref, v_ref, *rest, for_attn):
    if for_attn:
        qb_ref, kb_ref, vt_ref, ybuf, sem = rest
    else:
        q_ref, ybuf, sem = rest
    tm = x1_ref.shape[0]
    x2 = x1_ref[...] + _gather_combine(dest_ref, y_hbm, ybuf, sem, gate_ref[...], tm)
    x2_ref[...] = x2
    hk = _rms(x2, gkv_ref[...])
    k = _mm(hk, wk_ref[...])
    v = _mm(hk, wv_ref[...])
    k_ref[...] = k
    v_ref[...] = v
    q = _mm(_rms(x2, gmix_ref[...]), wq_ref[...])
    if for_attn:
        qb_ref[...] = (q * (ATTN_SCALE * LOG2E)).astype(BF16)
        kb_ref[...] = k.astype(BF16)
        vt_ref[:, 0] = v.T.reshape(vt_ref.shape[0], vt_ref.shape[2], tm).astype(BF16)
    else:
        q_ref[...] = q


def _combine_kvq(x1, gates, y_rows, dest, p, *, for_attn, tm):
    t, d = x1.shape
    dest3 = dest.reshape(t // tm, 1, tm * TOP_K)
    args = [dest3, x1, gates, y_rows, p['g_kv'][None, :], p['w_k'].astype(BF16), p['w_v'].astype(BF16),
            p['g_mix'][N_A_LAYERS][None, :], p['b_w_q'][0].astype(BF16)]
    tile = pl.BlockSpec((tm, d), lambda i: (i, 0))
    in_specs = [pl.BlockSpec((1, 1, tm * TOP_K), lambda i: (i, 0, 0), memory_space=pltpu.SMEM),
                tile, pl.BlockSpec((tm, TOP_K), lambda i: (i, 0)), pl.BlockSpec(memory_space=pl.ANY)]
    in_specs += [_full(a.shape) for a in args[4:]]
    out_shape = [jax.ShapeDtypeStruct((t, d), F32)] * 3
    out_specs = [tile] * 3
    if for_attn:
        n_h = d // V7X_LANES
        out_shape += [jax.ShapeDtypeStruct((t, d), BF16), jax.ShapeDtypeStruct((t, d), BF16),
                      jax.ShapeDtypeStruct((n_h, t // tm, V7X_LANES, tm), BF16)]
        out_specs += [tile, tile, pl.BlockSpec((n_h, 1, V7X_LANES, tm), lambda i: (0, i, 0, 0))]
    else:
        out_shape += [jax.ShapeDtypeStruct((t, d), F32)]
        out_specs += [tile]
    return pl.pallas_call(
        functools.partial(_combine_kvq_kernel, for_attn=for_attn),
        grid=(t // tm,), in_specs=in_specs, out_specs=out_specs, out_shape=out_shape,
        scratch_shapes=[pltpu.VMEM((TOP_K, tm, d), F32), pltpu.SemaphoreType.DMA(())],
        compiler_params=_cparams(1), name="moe_combine_kvq",
    )(*args)


def _bucket_np(n):
    n = np.asarray(n)
    nf = np.maximum(n, 1).astype(np.float32)
    large = MAX_EXACT + (np.log(nf / np.float32(MAX_EXACT)) / np.float32(math.log(MAX_DISTANCE / MAX_EXACT))
                         * np.float32(N_BUCKETS - MAX_EXACT)).astype(np.int32)
    return np.where(n < MAX_EXACT, n, np.minimum(large, N_BUCKETS - 1)).astype(np.int32)


def _bias_tiles_kernel(tbl_ref, bkt_ref, bkt_s_ref, tile_ref, last_ref, zero_ref):
    h = pl.program_id(0)
    far = tbl_ref[N_BUCKETS - 1, h]
    for dl in range(2):
        b = bkt_ref[dl]
        acc = jnp.where(b < 0, NEG_BIG, 0.0).astype(F32)
        for bb in range(N_BUCKETS - 1):
            acc = jnp.where(b == bb, (tbl_ref[bb, h] - far) * LOG2E, acc)
        tile_ref[0, dl] = acc

    @pl.when(h == 0)
    def _():
        last_ref[...] = jnp.zeros_like(last_ref)
        zero_ref[...] = jnp.zeros_like(zero_ref)

    bs = bkt_s_ref[...]
    val = jnp.zeros(bs.shape, F32)
    for bb in range(N_BUCKETS - 1):
        val = jnp.where(bs == bb, tbl_ref[bb, h] - far, val)
    row = lax.broadcasted_iota(jnp.int32, last_ref.shape, 0) % N_HEADS
    last_ref[...] = jnp.where(row == h, val, last_ref[...])
    zero_ref[...] = jnp.where(row == h, tbl_ref[0, h] - far, zero_ref[...])


def _bias_tiles(rel_table, tb, page):
    assert tb >= MAX_DISTANCE and page >= MAX_DISTANCE and page == V7X_LANES
    kq = np.arange(tb)
    tiles = []
    for dl in range(2):
        dist = dl * tb + kq[None, :] - kq[:, None]
        tiles.append(np.where(dist >= 0, _bucket_np(np.maximum(dist, 0)), -1))
    bkt = jnp.asarray(np.stack(tiles).astype(np.int32))
    bkt_s = jnp.asarray(_bucket_np(page - np.arange(page))[None, :].astype(np.int32))
    rows = (2 * N_HEADS, V7X_LANES)
    return pl.pallas_call(
        _bias_tiles_kernel,
        grid=(N_HEADS,),
        in_specs=[pl.BlockSpec(memory_space=pltpu.SMEM), _full(bkt.shape), _full(bkt_s.shape)],
        out_specs=[pl.BlockSpec((1, 2, tb, tb), lambda h: (h, 0, 0, 0)),
                   pl.BlockSpec(rows, lambda h: (0, 0)), pl.BlockSpec(rows, lambda h: (0, 0))],
        out_shape=[jax.ShapeDtypeStruct((N_HEADS, 2, tb, tb), F32),
                   jax.ShapeDtypeStruct(rows, F32), jax.ShapeDtypeStruct(rows, F32)],
        compiler_params=_cparams(1), name="rel_bias_tiles",
    )(rel_table, bkt, bkt_s)


def _attn_prompt_kernel(lam_ref, q_ref, k_ref, vt_ref, bias_ref, o_ref, acc_ref, m_ref, l_ref, *, tb):
    qi = pl.program_id(1)
    q = q_ref[...]
    lane = lax.broadcasted_iota(jnp.int32, q.shape, 1)
    zero = jnp.zeros_like(q)
    qcat = jnp.concatenate([jnp.where(lane < HEAD_DIM, q, zero), jnp.where(lane >= HEAD_DIM, q, zero)], axis=0)
    acc_ref[...] = jnp.zeros_like(acc_ref)

    def step(j, m_old, l_old, bias):
        k_blk = k_ref[pl.ds(pl.multiple_of(j * tb, tb), tb), :]
        s = lax.dot_general(k_blk, qcat, (((1,), (1,)), ((), ())), preferred_element_type=F32)
        if bias is not None:
            s = s + jnp.concatenate([bias, bias], axis=1)
        m_new = jnp.maximum(m_old, jnp.max(s, axis=0, keepdims=True))
        alpha = jnp.exp2(m_old - m_new)
        pr = jnp.exp2(s - m_new)
        l_new = alpha * l_old + jnp.sum(pr, axis=0, keepdims=True)
        acc_ref[...] = alpha * acc_ref[...] + jnp.dot(vt_ref[0, j], pr.astype(BF16), preferred_element_type=F32)
        return m_new, l_new

    init = (jnp.full((1, 2 * tb), NEG_BIG, F32), jnp.zeros((1, 2 * tb), F32))
    m_far, l_far = lax.fori_loop(0, jnp.maximum(qi - 1, 0), lambda j, c: step(j, c[0], c[1], None), init)
    m_ref[...] = m_far
    l_ref[...] = l_far

    @pl.when(qi >= 1)
    def _():
        m_ref[...], l_ref[...] = step(qi - 1, m_ref[...], l_ref[...], bias_ref[0, 1])
    _, l_fin = step(qi, m_ref[...], l_ref[...], bias_ref[0, 0])
    o_all = acc_ref[...] / l_fin
    o_ref[...] = (o_all[:, :tb] - lam_ref[0] * o_all[:, tb:]).T


def _attn_prompt(qb, kb, vt4, bias_tiles, lam, tb):
    t, d = qb.shape
    n_h = d // V7X_LANES
    n_q = t // tb
    return pl.pallas_call(
        functools.partial(_attn_prompt_kernel, tb=tb),
        grid=(n_h, n_q),
        in_specs=[pl.BlockSpec(memory_space=pltpu.SMEM),
                  pl.BlockSpec((tb, V7X_LANES), lambda h, i: (i, h)),
                  pl.BlockSpec((t, V7X_LANES), lambda h, i: (0, h)),
                  pl.BlockSpec((1, n_q, V7X_LANES, tb), lambda h, i: (h, 0, 0, 0)),
                  pl.BlockSpec((1, 2, tb, tb), lambda h, i: (h, 0, 0, 0))],
        out_specs=pl.BlockSpec((tb, V7X_LANES), lambda h, i: (i, h)),
        out_shape=jax.ShapeDtypeStruct((t, d), F32),
        scratch_shapes=[pltpu.VMEM((V7X_LANES, 2 * tb), F32), pltpu.VMEM((1, 2 * tb), F32),
                        pltpu.VMEM((1, 2 * tb), F32)],
        compiler_params=_cparams(2), name="attn_prompt",
    )(lam, qb, kb, vt4, bias_tiles)


def _attn_sample_kernel(pt_ref, lam_ref, qbd_ref, q_ref, kn_ref, vn_ref, hmask_ref, expand_ref, blast_ref,
                        bzero_ref, *rest, n_pp):
    k_refs = rest[:n_pp]
    v_refs = rest[n_pp:2 * n_pp]
    o_ref, acc_ref, m_ref, l_ref = rest[2 * n_pp:]
    del pt_ref
    g = pl.program_id(1)
    n_g = pl.num_programs(1)
    n_h = N_HEADS

    @pl.when(g == 0)
    def _():
        qv = q_ref[0].astype(BF16).astype(F32)
        s0 = jnp.sum(qv * kn_ref[0].astype(BF16).astype(F32), axis=-1, keepdims=True)
        m_ref[...] = s0 + bzero_ref[:, 0:1]
        l_ref[...] = jnp.ones_like(l_ref)
        vn = vn_ref[0].astype(BF16).astype(F32)
        acc_ref[...] = jnp.concatenate([vn, vn], axis=0)

    qbd = qbd_ref[0].astype(BF16)
    parts = []
    for i in range(n_pp):
        kst = k_refs[i][0]
        kst = kst.reshape(kst.shape[0] * kst.shape[1], kst.shape[2]).astype(BF16)
        parts.append(jnp.dot(qbd, kst, preferred_element_type=F32))
    parts[-1] = parts[-1] + jnp.where(g == n_g - 1, 1.0, 0.0) * blast_ref[...]
    s = jnp.concatenate(parts, axis=1)
    m_old = m_ref[...]
    m_new = jnp.maximum(m_old, jnp.max(s, axis=-1, keepdims=True))
    alpha = jnp.exp(m_old - m_new)
    pr = jnp.exp(s - m_new)
    l_ref[...] = alpha * l_ref[...] + jnp.sum(pr, axis=-1, keepdims=True)
    m_ref[...] = m_new
    keys = pr.shape[1] // n_pp
    acc = alpha * acc_ref[...]
    for i in range(n_pp):
        p_i = pr[:, i * keys:(i + 1) * keys].astype(BF16)
        spread = jnp.dot(p_i, expand_ref[...], preferred_element_type=F32) * hmask_ref[...]
        acc = acc + jnp.dot(spread.astype(BF16), v_refs[i][0].astype(BF16), preferred_element_type=F32)
    acc_ref[...] = acc

    @pl.when(g == n_g - 1)
    def _():
        o = acc_ref[...] / l_ref[...]
        o_ref[0] = o[:n_h] - lam_ref[0] * o[n_h:]


def _attn_sample(q, k_new, v_new, cache_k, cache_v, page_table, blast, bzero, lam):
    nb, d = q.shape
    n_phys, page = cache_k.shape[0], cache_k.shape[1]
    n_pages = page_table.shape[1]
    n_pp = math.gcd(PAGES_PER_STEP, n_pages)
    n_h = N_HEADS
    n_m = 2 * n_h
    perm = np.concatenate([np.arange(0, n_m, 2), np.arange(1, n_m, 2)])
    qp = (q * ATTN_SCALE).reshape(nb, n_m, HEAD_DIM)[:, perm]
    kn = k_new.reshape(nb, n_m, HEAD_DIM)[:, perm]
    vn = v_new.reshape(nb, n_h, V_DIM)
    own = jnp.asarray((perm[:, None] == np.arange(n_m)[None, :])[None, :, :, None])
    qbd = jnp.where(own, qp[:, :, None, :], 0.0).reshape(nb, n_m, n_m * HEAD_DIM)
    ckt = jnp.transpose(cache_k, (0, 2, 3, 1))
    cv = cache_v.reshape(n_phys, page * n_h, V_DIM)
    lanes = np.arange(page * n_h)
    hmask = jnp.asarray((lanes[None, :] % n_h == np.arange(n_m)[:, None] % n_h).astype(np.float32))
    expand = jnp.asarray((lanes[None, :] // n_h == np.arange(page)[:, None]).astype(np.float32), dtype=BF16)

    def page_map(i):
        return lambda b, g, pt: (pt[b * n_pages + g * n_pp + i], 0, 0, 0)

    def page_map3(i):
        return lambda b, g, pt: (pt[b * n_pages + g * n_pp + i], 0, 0)
    row3 = lambda b, g, pt: (b, 0, 0)
    const2 = lambda b, g, pt: (0, 0)
    in_specs = [pl.BlockSpec(memory_space=pltpu.SMEM),
                pl.BlockSpec((1, n_m, n_m * HEAD_DIM), row3),
                pl.BlockSpec((1, n_m, HEAD_DIM), row3), pl.BlockSpec((1, n_m, HEAD_DIM), row3),
                pl.BlockSpec((1, n_h, V_DIM), row3), pl.BlockSpec(hmask.shape, const2),
                pl.BlockSpec(expand.shape, const2), pl.BlockSpec(blast.shape, const2),
                pl.BlockSpec(bzero.shape, const2)]
    in_specs += [pl.BlockSpec((1, n_m, HEAD_DIM, page), page_map(i)) for i in range(n_pp)]
    in_specs += [pl.BlockSpec((1, page * n_h, V_DIM), page_map3(i)) for i in range(n_pp)]
    out = pl.pallas_call(
        functools.partial(_attn_sample_kernel, n_pp=n_pp),
        grid_spec=pltpu.PrefetchScalarGridSpec(
            num_scalar_prefetch=1, grid=(nb, n_pages // n_pp), in_specs=in_specs,
            out_specs=pl.BlockSpec((1, n_h, V_DIM), row3),
            scratch_shapes=[pltpu.VMEM((n_m, V_DIM), F32), pltpu.VMEM((n_m, 1), F32),
                            pltpu.VMEM((n_m, 1), F32)]),
        out_shape=jax.ShapeDtypeStruct((nb, n_h, V_DIM), F32),
        compiler_params=_cparams(2), name="attn_sample",
    )(page_table.reshape(-1).astype(jnp.int32), lam, qbd, qp, kn, vn, hmask, expand, blast, bzero,
      *([ckt] * n_pp), *([cv] * n_pp))
    return out.reshape(nb, d)


def _attn_out_kernel(o_ref, x2_ref, gsub_ref, wo_ref, gffn_ref, wr_ref, br_ref,
                     x3_ref, xn_ref, tope_ref, gate_ref, rank_ref, cnt_ref, runcnt_ref, on_ref,
                     *, sub_scale):
    @pl.when(pl.program_id(0) == 0)
    def _():
        runcnt_ref[...] = jnp.zeros_like(runcnt_ref)

    vd = gsub_ref.shape[1]
    for h in range(o_ref.shape[1] // vd):
        oh = o_ref[:, h * vd:(h + 1) * vd]
        on_ref[:, h * vd:(h + 1) * vd] = _rms(oh, gsub_ref[...]) * sub_scale
    x3 = x2_ref[...] + _mm(on_ref[...], wo_ref[...])
    x3_ref[...] = x3
    _router_tail(x3, gffn_ref[...], wr_ref[...], br_ref[...], runcnt_ref,
                 xn_ref, tope_ref, gate_ref, rank_ref, cnt_ref)


def _attn_out(o, x2, p, l, sub_scale, *, tm):
    t, d = x2.shape
    n_e = p['moe_w_router'].shape[-1]
    args = [o, x2, p['b_g_sub'][0][None, :], p['b_w_o'][0].astype(BF16), p['g_ffn'][l][None, :],
            p['moe_w_router'][l].astype(BF16), p['moe_b_router'][l][None, :]]
    tile = pl.BlockSpec((tm, d), lambda i: (i, 0))
    r_shapes, r_specs = _router_out_specs(t, tm, d, n_e)
    outs = pl.pallas_call(
        functools.partial(_attn_out_kernel, sub_scale=sub_scale),
        grid=(t // tm,), in_specs=[tile, tile] + [_full(a.shape) for a in args[2:]],
        out_specs=[tile, tile] + r_specs,
        out_shape=[jax.ShapeDtypeStruct((t, d), F32)] * 2 + r_shapes,
        scratch_shapes=[pltpu.VMEM((1, n_e), F32), pltpu.VMEM((tm, d), F32)],
        compiler_params=_cparams(1), name="attn_out_router",
    )(*args)
    return outs


def _combine_final_kernel(dest_ref, x3_ref, gate_ref, y_hbm, gout_ref, y_ref, ybuf, sem):
    tm = x3_ref.shape[0]
    x4 = x3_ref[...] + _gather_combine(dest_ref, y_hbm, ybuf, sem, gate_ref[...], tm)
    y_ref[...] = _rms(x4, gout_ref[...])


def _combine_final(x3, gates, y_rows, dest, g_out, tm):
    t, d = x3.shape
    dest3 = dest.reshape(t // tm, 1, tm * TOP_K)
    tile = pl.BlockSpec((tm, d), lambda i: (i, 0))
    return pl.pallas_call(
        _combine_final_kernel,
        grid=(t // tm,),
        in_specs=[pl.BlockSpec((1, 1, tm * TOP_K), lambda i: (i, 0, 0), memory_space=pltpu.SMEM),
                  tile, pl.BlockSpec((tm, TOP_K), lambda i: (i, 0)), pl.BlockSpec(memory_space=pl.ANY),
                  _full((1, d))],
        out_specs=tile, out_shape=jax.ShapeDtypeStruct((t, d), F32),
        scratch_shapes=[pltpu.VMEM((TOP_K, tm, d), F32), pltpu.SemaphoreType.DMA(())],
        compiler_params=_cparams(1), name="moe_combine_final",
    )(dest3, x3, gates, y_rows, g_out[None, :])


def _trunk(x, p, lam, sub_scale, attend, *, sample):
    t, d = x.shape
    tm = t if sample else PROMPT_TOKEN_TILE
    tb = SAMPLE_EXPERT_BLOCK if sample else PROMPT_EXPERT_BLOCK
    assert t % tm == 0 and (sample or tm % CHUNK == 0)
    x1, xn, v_rows, tope, gates, rank, cnt = _mixer_a(
        x, p, 0, chunked=not sample, want_v=sample, tm=tm)
    y_rows, dest = _moe_rows(xn, tope, rank, cnt, p, 0, tb, tm)
    outs = _combine_kvq(x1, gates, y_rows, dest, p, for_attn=not sample,
                        tm=tm if sample else min(ATTN_BLOCK, t))
    x2, k, v = outs[:3]
    o = attend(*outs[3:], k, v)
    x3, xn, tope, gates, rank, cnt = _attn_out(o, x2, p, 1, sub_scale, tm=tm)
    y_rows, dest = _moe_rows(xn, tope, rank, cnt, p, 1, tb, tm)
    y = _combine_final(x3, gates, y_rows, dest, p['g_out'], tm)
    return y, k, v, v_rows


def kernel(x_prompt, x_sample, cache_k, cache_v, page_table, g_mix, g_ffn, g_kv, g_out, a_w_in, a_b_in, a_g_v, a_b_v, a_w_s, a_b_s, a_w_out, w_k, w_v, b_w_q, b_lq1, b_lk1, b_lq2, b_lk2, b_g_sub, b_w_o, rel_table, moe_w_router, moe_b_router, moe_w_gu, moe_b_gu, moe_w_down, moe_b_down):
    p = {
        'g_mix': g_mix, 'g_ffn': g_ffn, 'g_kv': g_kv, 'g_out': g_out,
        'a_w_in': a_w_in, 'a_b_in': a_b_in, 'a_g_v': a_g_v, 'a_b_v': a_b_v,
        'a_w_s': a_w_s, 'a_b_s': a_b_s, 'a_w_out': a_w_out,
        'w_k': w_k, 'w_v': w_v, 'b_w_q': b_w_q, 'b_g_sub': b_g_sub, 'b_w_o': b_w_o,
        'moe_w_router': moe_w_router, 'moe_b_router': moe_b_router,
        'moe_w_gu': moe_w_gu, 'moe_b_gu': moe_b_gu, 'moe_w_down': moe_w_down, 'moe_b_down': moe_b_down,
    }
    assert x_prompt.shape[0] == 1 and x_sample.shape[1] == 1
    b, s, d = x_prompt.shape
    nb = x_sample.shape[0]
    page = cache_k.shape[1]
    lam_init = 0.8 - 0.6 * math.exp(-0.3 * N_A_LAYERS)
    lam = (jnp.exp(jnp.sum(b_lq1[0] * b_lk1[0])) - jnp.exp(jnp.sum(b_lq2[0] * b_lk2[0])) + lam_init)
    lam = lam.astype(F32)[None]
    tb = min(ATTN_BLOCK, s)
    bias_tiles, blast, bzero = _bias_tiles(rel_table, tb, page)

    def attend_prompt(qb, kb, vt4, k, v):
        return _attn_prompt(qb, kb, vt4, bias_tiles, lam, tb)

    def attend_sample(q, k, v):
        return _attn_sample(q, k, v, cache_k, cache_v, page_table, blast, bzero, lam)

    y_p, k_p, v_p, _ = _trunk(x_prompt[0], p, lam, 1.0 - lam_init, attend_prompt, sample=False)
    y_s, k_s, v_s, cv_s = _trunk(x_sample[:, 0], p, lam, 1.0 - lam_init, attend_sample, sample=True)
    n_h = N_HEADS
    return (y_p[None], y_s[:, None],
            k_p.reshape(1, s, 2 * n_h, HEAD_DIM), v_p.reshape(1, s, n_h, V_DIM),
            k_s.reshape(nb, 1, 2 * n_h, HEAD_DIM), v_s.reshape(nb, 1, n_h, V_DIM),
            cv_s.reshape(1, nb, 1, -1))
```

```python
import functools
import math

import numpy as np
import jax
import jax.numpy as jnp
from jax import lax
from jax.experimental import pallas as pl
from jax.experimental.pallas import tpu as pltpu

F32 = jnp.float32
BF16 = jnp.bfloat16

EPS = 1e-6
CHUNK = 128
A_GROUPS = 8
N_HEADS = 8
HEAD_DIM = 64
V_DIM = 2 * HEAD_DIM
ATTN_SCALE = HEAD_DIM ** -0.5
N_BUCKETS = 32
MAX_EXACT = N_BUCKETS // 2
MAX_DISTANCE = 128
TOP_K = 4
SWIGLU_LIMIT = 7.0
SWIGLU_ALPHA = 1.702
N_A_LAYERS = 1

V7X_LANES = 128
V7X_SUBLANES = 8
V7X_VMEM_LIMIT_BYTES = 56 * 1024 * 1024

NEG_BIG = -1e30
PROMPT_TOKEN_TILE = 256
PROMPT_EXPERT_BLOCK = 256
ATTN_BLOCK = 512
ATTN_ROW_CHUNK = 64
LOG2E = math.log2(math.e)
PAGES_PER_STEP = 8


def _cparams(n_axes):
    return pltpu.CompilerParams(
        dimension_semantics=("arbitrary",) * n_axes,
        vmem_limit_bytes=V7X_VMEM_LIMIT_BYTES,
    )


def _mm(a, w):
    return jnp.dot(a.astype(BF16), w.astype(BF16), preferred_element_type=F32)


def _rms(x, g):
    return x * lax.rsqrt(jnp.mean(x * x, axis=-1, keepdims=True) + EPS) * g


def _gelu(x):
    return 0.5 * x * (1.0 + lax.erf(x * (2.0 ** -0.5)))


def _cols4(c0, c1, c2, c3):
    m = c0.shape[0]
    lane = lax.broadcasted_iota(jnp.int32, (m, TOP_K), 1)
    return jnp.where(lane == 0, c0, jnp.where(lane == 1, c1, jnp.where(lane == 2, c2, c3)))


def _router_tail(x1, gffn, wr, br, runcnt_ref, xn_ref, tope_ref, gate_ref, rank_ref, cnt_ref):
    m = x1.shape[0]
    xn = _rms(x1, gffn)
    xn_ref[...] = xn
    logits = _mm(xn, wr) + br
    n_e = logits.shape[1]
    lane = lax.broadcasted_iota(jnp.int32, (m, n_e), 1).astype(F32)
    vals, sels, ohs = [], [], []
    cur = logits
    for _ in range(TOP_K):
        mx = jnp.max(cur, axis=-1, keepdims=True)
        sel = jnp.min(jnp.where(cur == mx, lane, float(n_e)), axis=-1, keepdims=True)
        oh = lane == sel
        vals.append(mx)
        sels.append(sel)
        ohs.append(oh.astype(F32))
        cur = jnp.where(oh, -jnp.inf, cur)
    es = [jnp.exp(v - vals[0]) for v in vals]
    den = es[0] + es[1] + es[2] + es[3]
    gate_ref[...] = _cols4(*[e / den for e in es])
    tope_ref[...] = _cols4(*sels).astype(jnp.int32)
    row = lax.broadcasted_iota(jnp.int32, (m, m), 0)
    col = lax.broadcasted_iota(jnp.int32, (m, m), 1)
    ltri = (row > col).astype(BF16)
    prior = runcnt_ref[...]
    ranks = []
    for oh in ohs:
        within = jnp.dot(ltri, oh.astype(BF16), preferred_element_type=F32)
        ranks.append(jnp.sum((prior + within) * oh, axis=-1, keepdims=True))
        prior = prior + jnp.sum(oh, axis=0, keepdims=True)
    rank_ref[...] = _cols4(*ranks).astype(jnp.int32)
    runcnt_ref[...] = prior
    cnt_ref[...] = prior.astype(jnp.int32)


def _gather_combine(dest_ref, y_hbm, ybuf, sem, gates, tm):
    def issue(t, carry):
        for k in range(TOP_K):
            d = dest_ref[0, 0, t * TOP_K + k]
            pltpu.make_async_copy(y_hbm.at[pl.ds(d, 1)], ybuf.at[k, pl.ds(t, 1)], sem).start()
        return carry
    lax.fori_loop(0, tm, issue, 0)
    for k in range(TOP_K):
        pltpu.make_async_copy(y_hbm.at[pl.ds(0, tm)], ybuf.at[k], sem).wait()
    acc = gates[:, 0:1] * ybuf[0]
    for k in range(1, TOP_K):
        acc = acc + gates[:, k:k + 1] * ybuf[k]
    return acc


def _mixer_a_kernel(*refs, chunked, want_v):
    (x_ref, gmix_ref, win_ref, bin_ref, gv_ref, bv_ref, ws_ref, bs_ref, wout_ref,
     gffn_ref, wr_ref, br_ref) = refs[:12]
    rest = list(refs[12:])
    x1_ref, xn_ref = rest[0], rest[1]
    rest = rest[2:]
    v_ref = rest.pop(0) if want_v else None
    tope_ref, gate_ref, rank_ref, cnt_ref, runcnt_ref = rest[:5]
    us_ref = rest[5] if chunked else None

    @pl.when(pl.program_id(0) == 0)
    def _():
        runcnt_ref[...] = jnp.zeros_like(runcnt_ref)

    x = x_ref[...]
    tm = x.shape[0]
    aw = wout_ref.shape[0]
    h = _rms(x, gmix_ref[...])
    u = _gelu(_mm(h, win_ref[:, :aw]) + bin_ref[:, :aw])
    vr = _gelu(_mm(h, win_ref[:, aw:]) + bin_ref[:, aw:])
    mu = jnp.mean(vr, axis=-1, keepdims=True)
    vc = vr - mu
    v = vc * lax.rsqrt(jnp.mean(vc * vc, axis=-1, keepdims=True) + EPS) * gv_ref[...] + bv_ref[...]
    if want_v:
        v_ref[...] = v
    if chunked:
        n_g, cl = ws_ref.shape[0], ws_ref.shape[1]
        gd = aw // n_g
        tri = (lax.broadcasted_iota(jnp.int32, (cl, cl), 0) >= lax.broadcasted_iota(jnp.int32, (cl, cl), 1))
        for g in range(n_g):
            wm = jnp.where(tri, ws_ref[g], 0.0).astype(BF16)
            for c in range(tm // cl):
                vg = v[c * cl:(c + 1) * cl, g * gd:(g + 1) * gd].astype(BF16)
                s = jnp.dot(wm, vg, preferred_element_type=F32) + bs_ref[g]
                us_ref[c * cl:(c + 1) * cl, g * gd:(g + 1) * gd] = u[c * cl:(c + 1) * cl, g * gd:(g + 1) * gd] * s
        us = us_ref[...]
    else:
        us = u * (v * ws_ref[...] + bs_ref[...])
    x1 = x + _mm(us, wout_ref[...])
    x1_ref[...] = x1
    _router_tail(x1, gffn_ref[...], wr_ref[...], br_ref[...], runcnt_ref,
                 xn_ref, tope_ref, gate_ref, rank_ref, cnt_ref)


def _full(shape):
    nd = len(shape)
    return pl.BlockSpec(shape, lambda *_: (0,) * nd, pipeline_mode=pl.Buffered(1))


def _router_out_specs(t, tm, d, n_e):
    shapes = [jax.ShapeDtypeStruct((t, TOP_K), jnp.int32), jax.ShapeDtypeStruct((t, TOP_K), F32),
              jax.ShapeDtypeStruct((t, TOP_K), jnp.int32), jax.ShapeDtypeStruct((1, n_e), jnp.int32)]
    specs = [pl.BlockSpec((tm, TOP_K), lambda i: (i, 0)), pl.BlockSpec((tm, TOP_K), lambda i: (i, 0)),
             pl.BlockSpec((tm, TOP_K), lambda i: (i, 0)), pl.BlockSpec((1, n_e), lambda i: (0, 0))]
    return shapes, specs


def _mixer_a(x, p, l, *, chunked, want_v, tm):
    t, d = x.shape
    w_in = p['a_w_in'][l].astype(BF16)
    w_out = p['a_w_out'][l].astype(BF16)
    aw = w_out.shape[0]
    n_e = p['moe_w_router'].shape[-1]
    if chunked:
        ws = p['a_w_s'][l]
        bs = p['a_b_s'][l][:, :, None]
    else:
        gd = aw // A_GROUPS
        ws = jnp.repeat(p['a_w_s'][l][:, 0, 0], gd)[None, :]
        bs = jnp.repeat(p['a_b_s'][l][:, 0], gd)[None, :]
    args = [x, p['g_mix'][l][None, :], w_in, p['a_b_in'][l][None, :], p['a_g_v'][l][None, :],
            p['a_b_v'][l][None, :], ws, bs, w_out, p['g_ffn'][l][None, :],
            p['moe_w_router'][l].astype(BF16), p['moe_b_router'][l][None, :]]
    in_specs = [pl.BlockSpec((tm, d), lambda i: (i, 0))] + [_full(a.shape) for a in args[1:]]
    r_shapes, r_specs = _router_out_specs(t, tm, d, n_e)
    out_shape = [jax.ShapeDtypeStruct((t, d), F32), jax.ShapeDtypeStruct((t, d), F32)]
    out_specs = [pl.BlockSpec((tm, d), lambda i: (i, 0)), pl.BlockSpec((tm, d), lambda i: (i, 0))]
    if want_v:
        out_shape.append(jax.ShapeDtypeStruct((t, aw), F32))
        out_specs.append(pl.BlockSpec((tm, aw), lambda i: (i, 0)))
    out_shape += r_shapes
    out_specs += r_specs
    scratch = [pltpu.VMEM((1, n_e), F32)]
    if chunked:
        scratch.append(pltpu.VMEM((tm, aw), F32))
    outs = pl.pallas_call(
        functools.partial(_mixer_a_kernel, chunked=chunked, want_v=want_v),
        grid=(t // tm,), in_specs=in_specs, out_specs=out_specs, out_shape=out_shape,
        scratch_shapes=scratch, compiler_params=_cparams(1), name="mixer_a_router",
    )(*args)
    outs = list(outs)
    x1, xn = outs[0], outs[1]
    v = outs[2] if want_v else None
    tope, gates, rank, cnt = outs[-4:]
    return x1, xn, v, tope, gates, rank, cnt


def _route_plan(routes, tb):
    n_e = routes[0][2].shape[1]
    n_tok = sum(r[0].shape[0] for r in routes)
    counts = sum(r[2][0] for r in routes)
    pc = (counts + tb - 1) // tb * tb
    pend = jnp.cumsum(pc)
    base = pend - pc
    dests = []
    for tope, rank, cnt in routes:
        dests.append((base[tope] + rank).astype(jnp.int32))
        base = base + cnt[0]
    n_blocks = -(-(n_tok * TOP_K) // tb) + n_e
    blk_start = jnp.arange(n_blocks, dtype=jnp.int32) * tb
    blk_e = jnp.minimum(jnp.sum(pend[None, :] <= blk_start[:, None], axis=1), n_e - 1).astype(jnp.int32)
    n_used = (pend[-1] // tb).astype(jnp.int32)[None]
    return dests, blk_e, n_used, n_blocks


def _dispatch_kernel(dest_ref, xn_ref, xs_in_hbm, xs_hbm, sem, *, td):
    del xs_in_hbm

    def issue(t, carry):
        for k in range(TOP_K):
            d = dest_ref[0, 0, t * TOP_K + k]
            pltpu.make_async_copy(xn_ref.at[pl.ds(t, 1)], xs_hbm.at[pl.ds(d, 1)], sem).start()
        return carry
    lax.fori_loop(0, td, issue, 0)
    for _ in range(TOP_K):
        pltpu.make_async_copy(xn_ref, xs_hbm.at[pl.ds(0, td)], sem).wait()


def _dispatch(xn, dest, rows, td):
    t, d = xn.shape
    dest3 = dest.reshape(t // td, 1, td * TOP_K)
    return pl.pallas_call(
        functools.partial(_dispatch_kernel, td=td),
        grid=(t // td,),
        in_specs=[pl.BlockSpec((1, 1, td * TOP_K), lambda i: (i, 0, 0), memory_space=pltpu.SMEM),
                  pl.BlockSpec((td, d), lambda i: (i, 0)), pl.BlockSpec(memory_space=pl.ANY)],
        out_specs=pl.BlockSpec(memory_space=pl.ANY),
        out_shape=jax.ShapeDtypeStruct(rows.shape, F32),
        scratch_shapes=[pltpu.SemaphoreType.DMA(())],
        input_output_aliases={2: 0},
        compiler_params=_cparams(1), name="moe_dispatch",
    )(dest3, xn, rows)


def _experts_kernel(be_ref, nu_ref, xs_ref, wgu_ref, bgu_ref, wdn_ref, bdn_ref, y_ref, wgu_bf, wdn_bf):
    i = pl.program_id(0)
    live = i < nu_ref[0]
    f = wdn_ref.shape[2]
    changed = jnp.logical_or(i == 0, be_ref[i] != be_ref[jnp.maximum(i - 1, 0)])

    @pl.when(jnp.logical_and(live, changed))
    def _():
        wgu_bf[...] = wgu_ref[0, 0].astype(BF16)
        wdn_bf[...] = wdn_ref[0, 0].astype(BF16)

    @pl.when(live)
    def _():
        x = xs_ref[...]
        hgu = _mm(x, wgu_bf[...]) + bgu_ref[0, 0]
        gate = jnp.minimum(hgu[:, :f], SWIGLU_LIMIT)
        up = jnp.clip(hgu[:, f:], -SWIGLU_LIMIT, SWIGLU_LIMIT)
        act = gate * jax.nn.sigmoid(SWIGLU_ALPHA * gate) * (up + 1.0)
        y_ref[...] = _mm(act, wdn_bf[...]) + bdn_ref[0, 0]

    @pl.when(jnp.logical_not(live))
    def _():
        y_ref[...] = jnp.zeros_like(y_ref)


def _experts(xs, blk_e, n_used, n_blocks, p, l, tb):
    n_rows, d = xs.shape
    wgu, wdn = p['moe_w_gu'], p['moe_w_down']
    _, n_e, _, f2 = wgu.shape
    f = f2 // 2
    bgu = p['moe_b_gu'][:, :, None, :]
    bdn = p['moe_b_down'][:, :, None, :]

    def row_map(i, be, nu):
        return (jnp.minimum(i, nu[0] - 1), 0)

    def w_map(i, be, nu):
        return (l, be[jnp.minimum(i, nu[0] - 1)], 0, 0)

    return pl.pallas_call(
        _experts_kernel,
        grid_spec=pltpu.PrefetchScalarGridSpec(
            num_scalar_prefetch=2, grid=(n_blocks,),
            in_specs=[pl.BlockSpec((tb, d), row_map), pl.BlockSpec((1, 1, d, f2), w_map),
                      pl.BlockSpec((1, 1, 1, f2), w_map), pl.BlockSpec((1, 1, f, d), w_map),
                      pl.BlockSpec((1, 1, 1, d), w_map)],
            out_specs=pl.BlockSpec((tb, d), lambda i, be, nu: (i, 0)),
            scratch_shapes=[pltpu.VMEM((d, f2), BF16), pltpu.VMEM((f, d), BF16)]),
        out_shape=jax.ShapeDtypeStruct((n_rows, d), F32),
        compiler_params=_cparams(1), name="moe_experts",
    )(blk_e, n_used, xs, wgu, bgu, wdn, bdn)


def _moe_rows(groups, p, l, spare_rows=None):
    tb = PROMPT_EXPERT_BLOCK
    dests, blk_e, n_used, n_blocks = _route_plan([g[1:4] for g in groups], tb)
    xs = jnp.zeros((n_blocks * tb, groups[0][0].shape[1]), F32) if spare_rows is None else spare_rows
    for (xn, _, _, _, td), dest in zip(groups, dests):
        xs = _dispatch(xn, dest, xs, td)
    return _experts(xs, blk_e, n_used, n_blocks, p, l, tb), dests, xs


def _combine_kvq_kernel(dest_ref, x1_ref, gate_ref, y_hbm, gkv_ref, wk_ref, wv_ref, gmix_ref, wq_ref,
                        x2_ref, k_ref, v_ref, *rest, for_attn):
    if for_attn:
        qb_ref, kb_ref, vt_ref, ybuf, sem = rest
    else:
        q_ref, ybuf, sem = rest
    tm = x1_ref.shape[0]
    x2 = x1_ref[...] + _gather_combine(dest_ref, y_hbm, ybuf, sem, gate_ref[...], tm)
    x2_ref[...] = x2
    hk = _rms(x2, gkv_ref[...])
    k = _mm(hk, wk_ref[...])
    v = _mm(hk, wv_ref[...])
    k_ref[...] = k
    v_ref[...] = v
    q = _mm(_rms(x2, gmix_ref[...]), wq_ref[...])
    if for_attn:
        qb_ref[...] = (q * (ATTN_SCALE * LOG2E)).astype(BF16)
        kb_ref[...] = k.astype(BF16)
        vt_ref[:, 0] = v.T.reshape(vt_ref.shape[0], vt_ref.shape[2], tm).astype(BF16)
    else:
        q_ref[...] = q


def _combine_kvq(x1, gates, y_rows, dest, p, *, for_attn, tm):
    t, d = x1.shape
    dest3 = dest.reshape(t // tm, 1, tm * TOP_K)
    args = [dest3, x1, gates, y_rows, p['g_kv'][None, :], p['w_k'].astype(BF16), p['w_v'].astype(BF16),
            p['g_mix'][N_A_LAYERS][None, :], p['b_w_q'][0].astype(BF16)]
    tile = pl.BlockSpec((tm, d), lambda i: (i, 0))
    in_specs = [pl.BlockSpec((1, 1, tm * TOP_K), lambda i: (i, 0, 0), memory_space=pltpu.SMEM),
                tile, pl.BlockSpec((tm, TOP_K), lambda i: (i, 0)), pl.BlockSpec(memory_space=pl.ANY)]
    in_specs += [_full(a.shape) for a in args[4:]]
    out_shape = [jax.ShapeDtypeStruct((t, d), F32)] * 3
    out_specs = [tile] * 3
    if for_attn:
        n_h = d // V7X_LANES
        out_shape += [jax.ShapeDtypeStruct((t, d), BF16), jax.ShapeDtypeStruct((t, d), BF16),
                      jax.ShapeDtypeStruct((n_h, t // tm, V7X_LANES, tm), BF16)]
        out_specs += [tile, tile, pl.BlockSpec((n_h, 1, V7X_LANES, tm), lambda i: (0, i, 0, 0))]
    else:
        out_shape += [jax.ShapeDtypeStruct((t, d), F32)]
        out_specs += [tile]
    return pl.pallas_call(
        functools.partial(_combine_kvq_kernel, for_attn=for_attn),
        grid=(t // tm,), in_specs=in_specs, out_specs=out_specs, out_shape=out_shape,
        scratch_shapes=[pltpu.VMEM((TOP_K, tm, d), F32), pltpu.SemaphoreType.DMA(())],
        compiler_params=_cparams(1), name="moe_combine_kvq",
    )(*args)


def _bucket_np(n):
    n = np.asarray(n)
    nf = np.maximum(n, 1).astype(np.float32)
    large = MAX_EXACT + (np.log(nf / np.float32(MAX_EXACT)) / np.float32(math.log(MAX_DISTANCE / MAX_EXACT))
                         * np.float32(N_BUCKETS - MAX_EXACT)).astype(np.int32)
    return np.where(n < MAX_EXACT, n, np.minimum(large, N_BUCKETS - 1)).astype(np.int32)


def _bias_tiles_kernel(tbl_ref, bkt_ref, bkt_s_ref, tile_ref, last_ref, zero_ref):
    h = pl.program_id(0)
    far = tbl_ref[N_BUCKETS - 1, h]
    for dl in range(2):
        b = bkt_ref[dl]
        acc = jnp.where(b < 0, NEG_BIG, 0.0).astype(F32)
        for bb in range(N_BUCKETS - 1):
            acc = jnp.where(b == bb, (tbl_ref[bb, h] - far) * LOG2E, acc)
        tile_ref[0, dl] = acc

    @pl.when(h == 0)
    def _():
        last_ref[...] = jnp.zeros_like(last_ref)
        zero_ref[...] = jnp.zeros_like(zero_ref)

    bs = bkt_s_ref[...]
    val = jnp.zeros(bs.shape, F32)
    for bb in range(N_BUCKETS - 1):
        val = jnp.where(bs == bb, tbl_ref[bb, h] - far, val)
    row = lax.broadcasted_iota(jnp.int32, last_ref.shape, 0) % N_HEADS
    last_ref[...] = jnp.where(row == h, val, last_ref[...])
    zero_ref[...] = jnp.where(row == h, tbl_ref[0, h] - far, zero_ref[...])


def _bias_tiles(rel_table, tb, page):
    assert tb >= MAX_DISTANCE and page >= MAX_DISTANCE and page == V7X_LANES
    kq = np.arange(tb)
    tiles = []
    for dl in range(2):
        dist = dl * tb + kq[None, :] - kq[:, None]
        tiles.append(np.where(dist >= 0, _bucket_np(np.maximum(dist, 0)), -1))
    bkt = jnp.asarray(np.stack(tiles).astype(np.int32))
    bkt_s = jnp.asarray(_bucket_np(page - np.arange(page))[None, :].astype(np.int32))
    rows = (2 * N_HEADS, V7X_LANES)
    return pl.pallas_call(
        _bias_tiles_kernel,
        grid=(N_HEADS,),
        in_specs=[pl.BlockSpec(memory_space=pltpu.SMEM), _full(bkt.shape), _full(bkt_s.shape)],
        out_specs=[pl.BlockSpec((1, 2, tb, tb), lambda h: (h, 0, 0, 0)),
                   pl.BlockSpec(rows, lambda h: (0, 0)), pl.BlockSpec(rows, lambda h: (0, 0))],
        out_shape=[jax.ShapeDtypeStruct((N_HEADS, 2, tb, tb), F32),
                   jax.ShapeDtypeStruct(rows, F32), jax.ShapeDtypeStruct(rows, F32)],
        compiler_params=_cparams(1), name="rel_bias_tiles",
    )(rel_table, bkt, bkt_s)


def _attn_prompt_kernel(lam_ref, q_ref, k_ref, vt_ref, bias_ref, o_ref, acc_ref, m_ref, l_ref, sa_ref, sb_ref,
                        pa_ref, pb_ref, *, tb):
    qi = pl.program_id(1)
    q = q_ref[...]
    lane = lax.broadcasted_iota(jnp.int32, q.shape, 1)
    zero = jnp.zeros_like(q)
    qcat = jnp.concatenate([jnp.where(lane < HEAD_DIM, q, zero), jnp.where(lane >= HEAD_DIM, q, zero)], axis=0)
    acc_ref[...] = jnp.zeros_like(acc_ref)

    def scores(j, bias=None):
        k_blk = k_ref[pl.ds(pl.multiple_of(j * tb, tb), tb), :]
        s = lax.dot_general(k_blk, qcat, (((1,), (1,)), ((), ())), preferred_element_type=F32)
        if bias is not None:
            s = s + jnp.concatenate([bias, bias], axis=1)
        return s, jnp.max(s, axis=0, keepdims=True)

    def absorb(s_ref, p_ref, s_max, j, m_old, l_old):
        m_new = jnp.maximum(m_old, s_max)
        alpha = jnp.exp2(m_old - m_new)
        part = jnp.zeros((V7X_SUBLANES, 2 * tb), F32)
        for c in range(0, tb, ATTN_ROW_CHUNK):
            pr = jnp.exp2(s_ref[c:c + ATTN_ROW_CHUNK, :] - m_new)
            part = part + jnp.sum(pr.reshape(-1, V7X_SUBLANES, 2 * tb), axis=0)
            p_ref[c:c + ATTN_ROW_CHUNK, :] = pr.astype(BF16)
        l_new = alpha * l_old + jnp.sum(part, axis=0, keepdims=True)
        acc_ref[...] = alpha * acc_ref[...] + jnp.dot(vt_ref[0, j], p_ref[...], preferred_element_type=F32)
        return m_new, l_new

    n_far = jnp.maximum(qi - 1, 0)
    n_pairs = lax.shift_right_logical(n_far, 1)
    neg = jnp.full((1, 2 * tb), NEG_BIG, F32)
    m_ref[...] = neg

    @pl.when(n_pairs > 0)
    def _():
        sa_ref[...], m_ref[...] = scores(0)

    def pair(i, carry):
        m_old, l_old, max_a = carry
        a = 2 * i
        sb_ref[...], max_b = scores(a + 1)
        m_old, l_old = absorb(sa_ref, pa_ref, max_a, a, m_old, l_old)
        sa_ref[...], max_a = scores(a + 2)
        m_old, l_old = absorb(sb_ref, pb_ref, max_b, a + 1, m_old, l_old)
        return m_old, l_old, max_a

    m_ref[...], l_ref[...], _ = lax.fori_loop(0, n_pairs, pair, (neg, jnp.zeros((1, 2 * tb), F32), m_ref[...]))

    def single(j, bias=None):
        sa_ref[...], s_max = scores(j, bias)
        return absorb(sa_ref, pa_ref, s_max, j, m_ref[...], l_ref[...])

    @pl.when(n_far > 2 * n_pairs)
    def _():
        m_ref[...], l_ref[...] = single(2 * n_pairs)

    @pl.when(qi >= 1)
    def _():
        m_ref[...], l_ref[...] = single(qi - 1, bias_ref[0, 1])
    _, l_fin = single(qi, bias_ref[0, 0])
    o_all = acc_ref[...] / l_fin
    o_ref[...] = (o_all[:, :tb] - lam_ref[0] * o_all[:, tb:]).T


def _attn_prompt(qb, kb, vt4, bias_tiles, lam, tb):
    t, d = qb.shape
    n_h = d // V7X_LANES
    n_q = t // tb
    return pl.pallas_call(
        functools.partial(_attn_prompt_kernel, tb=tb),
        grid=(n_h, n_q),
        in_specs=[pl.BlockSpec(memory_space=pltpu.SMEM),
                  pl.BlockSpec((tb, V7X_LANES), lambda h, i: (i, h)),
                  pl.BlockSpec((t, V7X_LANES), lambda h, i: (0, h)),
                  pl.BlockSpec((1, n_q, V7X_LANES, tb), lambda h, i: (h, 0, 0, 0)),
                  pl.BlockSpec((1, 2, tb, tb), lambda h, i: (h, 0, 0, 0))],
        out_specs=pl.BlockSpec((tb, V7X_LANES), lambda h, i: (i, h)),
        out_shape=jax.ShapeDtypeStruct((t, d), F32),
        scratch_shapes=[pltpu.VMEM((V7X_LANES, 2 * tb), F32), pltpu.VMEM((1, 2 * tb), F32),
                        pltpu.VMEM((1, 2 * tb), F32), pltpu.VMEM((tb, 2 * tb), F32),
                        pltpu.VMEM((tb, 2 * tb), F32), pltpu.VMEM((tb, 2 * tb), BF16),
                        pltpu.VMEM((tb, 2 * tb), BF16)],
        compiler_params=_cparams(2), name="attn_prompt",
    )(lam, qb, kb, vt4, bias_tiles)


def _attn_sample_kernel(pt_ref, lam_ref, qbd_ref, q_ref, kn_ref, vn_ref, hmask_ref, expand_ref, blast_ref,
                        bzero_ref, *rest, n_pp):
    k_refs = rest[:n_pp]
    v_refs = rest[n_pp:2 * n_pp]
    o_ref, acc_ref, m_ref, l_ref = rest[2 * n_pp:]
    del pt_ref
    g = pl.program_id(1)
    n_g = pl.num_programs(1)
    n_h = N_HEADS

    @pl.when(g == 0)
    def _():
        qv = q_ref[0].astype(BF16).astype(F32)
        s0 = jnp.sum(qv * kn_ref[0].astype(BF16).astype(F32), axis=-1, keepdims=True)
        m_ref[...] = s0 + bzero_ref[:, 0:1]
        l_ref[...] = jnp.ones_like(l_ref)
        vn = vn_ref[0].astype(BF16).astype(F32)
        acc_ref[...] = jnp.concatenate([vn, vn], axis=0)

    qbd = qbd_ref[0].astype(BF16)
    parts = []
    for i in range(n_pp):
        kst = k_refs[i][0]
        kst = kst.reshape(kst.shape[0] * kst.shape[1], kst.shape[2]).astype(BF16)
        parts.append(jnp.dot(qbd, kst, preferred_element_type=F32))
    parts[-1] = parts[-1] + jnp.where(g == n_g - 1, 1.0, 0.0) * blast_ref[...]
    s = jnp.concatenate(parts, axis=1)
    m_old = m_ref[...]
    m_new = jnp.maximum(m_old, jnp.max(s, axis=-1, keepdims=True))
    alpha = jnp.exp(m_old - m_new)
    pr = jnp.exp(s - m_new)
    l_ref[...] = alpha * l_ref[...] + jnp.sum(pr, axis=-1, keepdims=True)
    m_ref[...] = m_new
    keys = pr.shape[1] // n_pp
    n_m = pr.shape[0]
    p_rows = jnp.concatenate([pr[:, i * keys:(i + 1) * keys] for i in range(n_pp)], axis=0).astype(BF16)
    spread = jnp.dot(p_rows, expand_ref[...], preferred_element_type=F32) * hmask_ref[...]
    spread = spread.astype(BF16)
    acc = alpha * acc_ref[...]
    for i in range(n_pp):
        acc = acc + jnp.dot(spread[i * n_m:(i + 1) * n_m], v_refs[i][0].astype(BF16),
                            preferred_element_type=F32)
    acc_ref[...] = acc

    @pl.when(g == n_g - 1)
    def _():
        o = acc_ref[...] / l_ref[...]
        o_ref[0] = o[:n_h] - lam_ref[0] * o[n_h:]


def _attn_sample(q, k_new, v_new, cache_k, cache_v, page_table, blast, bzero, lam):
    nb, d = q.shape
    n_phys, page = cache_k.shape[0], cache_k.shape[1]
    n_pages = page_table.shape[1]
    n_pp = math.gcd(PAGES_PER_STEP, n_pages)
    n_h = N_HEADS
    n_m = 2 * n_h
    perm = np.concatenate([np.arange(0, n_m, 2), np.arange(1, n_m, 2)])
    qp = (q * ATTN_SCALE).reshape(nb, n_m, HEAD_DIM)[:, perm]
    kn = k_new.reshape(nb, n_m, HEAD_DIM)[:, perm]
    vn = v_new.reshape(nb, n_h, V_DIM)
    own = jnp.asarray((perm[:, None] == np.arange(n_m)[None, :])[None, :, :, None])
    qbd = jnp.where(own, qp[:, :, None, :], 0.0).reshape(nb, n_m, n_m * HEAD_DIM)
    ckt = jnp.transpose(cache_k, (0, 2, 3, 1))
    cv = cache_v.reshape(n_phys, page * n_h, V_DIM)
    lanes = np.arange(page * n_h)
    hmask = jnp.asarray((lanes[None, :] % n_h == np.arange(n_pp * n_m)[:, None] % n_h).astype(np.float32))
    expand = jnp.asarray((lanes[None, :] // n_h == np.arange(page)[:, None]).astype(np.float32), dtype=BF16)

    def page_map(i):
        return lambda b, g, pt: (pt[b * n_pages + g * n_pp + i], 0, 0, 0)

    def page_map3(i):
        return lambda b, g, pt: (pt[b * n_pages + g * n_pp + i], 0, 0)
    row3 = lambda b, g, pt: (b, 0, 0)
    const2 = lambda b, g, pt: (0, 0)
    in_specs = [pl.BlockSpec(memory_space=pltpu.SMEM),
                pl.BlockSpec((1, n_m, n_m * HEAD_DIM), row3),
                pl.BlockSpec((1, n_m, HEAD_DIM), row3), pl.BlockSpec((1, n_m, HEAD_DIM), row3),
                pl.BlockSpec((1, n_h, V_DIM), row3), pl.BlockSpec(hmask.shape, const2),
                pl.BlockSpec(expand.shape, const2), pl.BlockSpec(blast.shape, const2),
                pl.BlockSpec(bzero.shape, const2)]
    in_specs += [pl.BlockSpec((1, n_m, HEAD_DIM, page), page_map(i)) for i in range(n_pp)]
    in_specs += [pl.BlockSpec((1, page * n_h, V_DIM), page_map3(i)) for i in range(n_pp)]
    out = pl.pallas_call(
        functools.partial(_attn_sample_kernel, n_pp=n_pp),
        grid_spec=pltpu.PrefetchScalarGridSpec(
            num_scalar_prefetch=1, grid=(nb, n_pages // n_pp), in_specs=in_specs,
            out_specs=pl.BlockSpec((1, n_h, V_DIM), row3),
            scratch_shapes=[pltpu.VMEM((n_m, V_DIM), F32), pltpu.VMEM((n_m, 1), F32),
                            pltpu.VMEM((n_m, 1), F32)]),
        out_shape=jax.ShapeDtypeStruct((nb, n_h, V_DIM), F32),
        compiler_params=_cparams(2), name="attn_sample",
    )(page_table.reshape(-1).astype(jnp.int32), lam, qbd, qp, kn, vn, hmask, expand, blast, bzero,
      *([ckt] * n_pp), *([cv] * n_pp))
    return out.reshape(nb, d)


def _attn_out_kernel(o_ref, x2_ref, gsub_ref, wo_ref, gffn_ref, wr_ref, br_ref,
                     x3_ref, xn_ref, tope_ref, gate_ref, rank_ref, cnt_ref, runcnt_ref, on_ref,
                     *, sub_scale):
    @pl.when(pl.program_id(0) == 0)
    def _():
        runcnt_ref[...] = jnp.zeros_like(runcnt_ref)

    vd = gsub_ref.shape[1]
    for h in range(o_ref.shape[1] // vd):
        oh = o_ref[:, h * vd:(h + 1) * vd]
        on_ref[:, h * vd:(h + 1) * vd] = _rms(oh, gsub_ref[...]) * sub_scale
    x3 = x2_ref[...] + _mm(on_ref[...], wo_ref[...])
    x3_ref[...] = x3
    _router_tail(x3, gffn_ref[...], wr_ref[...], br_ref[...], runcnt_ref,
                 xn_ref, tope_ref, gate_ref, rank_ref, cnt_ref)


def _attn_out(o, x2, p, l, sub_scale, *, tm):
    t, d = x2.shape
    n_e = p['moe_w_router'].shape[-1]
    args = [o, x2, p['b_g_sub'][0][None, :], p['b_w_o'][0].astype(BF16), p['g_ffn'][l][None, :],
            p['moe_w_router'][l].astype(BF16), p['moe_b_router'][l][None, :]]
    tile = pl.BlockSpec((tm, d), lambda i: (i, 0))
    r_shapes, r_specs = _router_out_specs(t, tm, d, n_e)
    outs = pl.pallas_call(
        functools.partial(_attn_out_kernel, sub_scale=sub_scale),
        grid=(t // tm,), in_specs=[tile, tile] + [_full(a.shape) for a in args[2:]],
        out_specs=[tile, tile] + r_specs,
        out_shape=[jax.ShapeDtypeStruct((t, d), F32)] * 2 + r_shapes,
        scratch_shapes=[pltpu.VMEM((1, n_e), F32), pltpu.VMEM((tm, d), F32)],
        compiler_params=_cparams(1), name="attn_out_router",
    )(*args)
    return outs


def _combine_final_kernel(dest_ref, x3_ref, gate_ref, y_hbm, gout_ref, y_ref, ybuf, sem):
    tm = x3_ref.shape[0]
    x4 = x3_ref[...] + _gather_combine(dest_ref, y_hbm, ybuf, sem, gate_ref[...], tm)
    y_ref[...] = _rms(x4, gout_ref[...])


def _combine_final(x3, gates, y_rows, dest, g_out, tm):
    t, d = x3.shape
    dest3 = dest.reshape(t // tm, 1, tm * TOP_K)
    tile = pl.BlockSpec((tm, d), lambda i: (i, 0))
    return pl.pallas_call(
        _combine_final_kernel,
        grid=(t // tm,),
        in_specs=[pl.BlockSpec((1, 1, tm * TOP_K), lambda i: (i, 0, 0), memory_space=pltpu.SMEM),
                  tile, pl.BlockSpec((tm, TOP_K), lambda i: (i, 0)), pl.BlockSpec(memory_space=pl.ANY),
                  _full((1, d))],
        out_specs=tile, out_shape=jax.ShapeDtypeStruct((t, d), F32),
        scratch_shapes=[pltpu.VMEM((TOP_K, tm, d), F32), pltpu.SemaphoreType.DMA(())],
        compiler_params=_cparams(1), name="moe_combine_final",
    )(dest3, x3, gates, y_rows, g_out[None, :])


def kernel(x_prompt, x_sample, cache_k, cache_v, page_table, g_mix, g_ffn, g_kv, g_out, a_w_in, a_b_in, a_g_v, a_b_v, a_w_s, a_b_s, a_w_out, w_k, w_v, b_w_q, b_lq1, b_lk1, b_lq2, b_lk2, b_g_sub, b_w_o, rel_table, moe_w_router, moe_b_router, moe_w_gu, moe_b_gu, moe_w_down, moe_b_down):
    p = {
        'g_mix': g_mix, 'g_ffn': g_ffn, 'g_kv': g_kv, 'g_out': g_out,
        'a_w_in': a_w_in, 'a_b_in': a_b_in, 'a_g_v': a_g_v, 'a_b_v': a_b_v,
        'a_w_s': a_w_s, 'a_b_s': a_b_s, 'a_w_out': a_w_out,
        'w_k': w_k, 'w_v': w_v, 'b_w_q': b_w_q, 'b_g_sub': b_g_sub, 'b_w_o': b_w_o,
        'moe_w_router': moe_w_router, 'moe_b_router': moe_b_router,
        'moe_w_gu': moe_w_gu, 'moe_b_gu': moe_b_gu, 'moe_w_down': moe_w_down, 'moe_b_down': moe_b_down,
    }
    assert x_prompt.shape[0] == 1 and x_sample.shape[1] == 1
    b, s, d = x_prompt.shape
    nb = x_sample.shape[0]
    page = cache_k.shape[1]
    lam_init = 0.8 - 0.6 * math.exp(-0.3 * N_A_LAYERS)
    lam = (jnp.exp(jnp.sum(b_lq1[0] * b_lk1[0])) - jnp.exp(jnp.sum(b_lq2[0] * b_lk2[0])) + lam_init)
    lam = lam.astype(F32)[None]
    tb = min(ATTN_BLOCK, s)
    bias_tiles, blast, bzero = _bias_tiles(rel_table, tb, page)
    tm_p, tm_s = PROMPT_TOKEN_TILE, nb
    assert s % tm_p == 0 and tm_p % CHUNK == 0 and s % tb == 0

    x1_p, xn_p, _, tope_p, gates_p, rank_p, cnt_p = _mixer_a(
        x_prompt[0], p, 0, chunked=True, want_v=False, tm=tm_p)
    x1_s, xn_s, cv_s, tope_s, gates_s, rank_s, cnt_s = _mixer_a(
        x_sample[:, 0], p, 0, chunked=False, want_v=True, tm=tm_s)
    y_rows, (dest_p, dest_s), rows_buf = _moe_rows(
        [(xn_p, tope_p, rank_p, cnt_p, tm_p), (xn_s, tope_s, rank_s, cnt_s, tm_s)], p, 0)
    x2_p, k_p, v_p, qb, kb, vt4 = _combine_kvq(x1_p, gates_p, y_rows, dest_p, p, for_attn=True, tm=tb)
    x2_s, k_s, v_s, q_s = _combine_kvq(x1_s, gates_s, y_rows, dest_s, p, for_attn=False, tm=tm_s)

    o_p = _attn_prompt(qb, kb, vt4, bias_tiles, lam, tb)
    o_s = _attn_sample(q_s, k_s, v_s, cache_k, cache_v, page_table, blast, bzero, lam)
    x3_p, xn_p, tope_p, gates_p, rank_p, cnt_p = _attn_out(o_p, x2_p, p, 1, 1.0 - lam_init, tm=tm_p)
    x3_s, xn_s, tope_s, gates_s, rank_s, cnt_s = _attn_out(o_s, x2_s, p, 1, 1.0 - lam_init, tm=tm_s)
    y_rows, (dest_p, dest_s), _ = _moe_rows(
        [(xn_p, tope_p, rank_p, cnt_p, tm_p), (xn_s, tope_s, rank_s, cnt_s, tm_s)], p, 1, rows_buf)
    y_p = _combine_final(x3_p, gates_p, y_rows, dest_p, g_out, tm_p)
    y_s = _combine_final(x3_s, gates_s, y_rows, dest_s, g_out, tm_s)
    n_h = N_HEADS
    return (y_p[None], y_s[:, None],
            k_p.reshape(1, s, 2 * n_h, HEAD_DIM), v_p.reshape(1, s, n_h, V_DIM),
            k_s.reshape(nb, 1, 2 * n_h, HEAD_DIM), v_s.reshape(nb, 1, n_h, V_DIM),
            cv_s.reshape(1, nb, 1, -1))
```

```python
import functools
import math

import numpy as np
import jax
import jax.numpy as jnp
from jax import lax
from jax.experimental import pallas as pl
from jax.experimental.pallas import tpu as pltpu

F32 = jnp.float32
BF16 = jnp.bfloat16

EPS = 1e-6
CHUNK = 128
A_GROUPS = 8
N_HEADS = 8
HEAD_DIM = 64
V_DIM = 2 * HEAD_DIM
ATTN_SCALE = HEAD_DIM ** -0.5
N_BUCKETS = 32
MAX_EXACT = N_BUCKETS // 2
MAX_DISTANCE = 128
TOP_K = 4
SWIGLU_LIMIT = 7.0
SWIGLU_ALPHA = 1.702
N_A_LAYERS = 1

V7X_LANES = 128
V7X_SUBLANES = 8
V7X_VMEM_LIMIT_BYTES = 56 * 1024 * 1024

NEG_BIG = -1e30
PROMPT_TOKEN_TILE = 256
PROMPT_EXPERT_BLOCK = 512
ATTN_BLOCK = 512
ATTN_ROW_CHUNK = 64
VT_PAD_ROWS = 16
LOG2E = math.log2(math.e)
PAGES_PER_STEP = 8


def _cparams(n_axes):
    return pltpu.CompilerParams(
        dimension_semantics=("arbitrary",) * n_axes,
        vmem_limit_bytes=V7X_VMEM_LIMIT_BYTES,
    )


def _mm(a, w):
    return jnp.dot(a.astype(BF16), w.astype(BF16), preferred_element_type=F32)


def _rms(x, g):
    return x * lax.rsqrt(jnp.mean(x * x, axis=-1, keepdims=True) + EPS) * g


def _gelu(x):
    return 0.5 * x * (1.0 + lax.erf(x * (2.0 ** -0.5)))


def _cols4(c0, c1, c2, c3):
    m = c0.shape[0]
    lane = lax.broadcasted_iota(jnp.int32, (m, TOP_K), 1)
    return jnp.where(lane == 0, c0, jnp.where(lane == 1, c1, jnp.where(lane == 2, c2, c3)))


def _router_tail(x1, gffn, wr, br, runcnt_ref, xn_ref, tope_ref, gate_ref, rank_ref, cnt_ref):
    m = x1.shape[0]
    xn = _rms(x1, gffn)
    xn_ref[...] = xn
    logits = _mm(xn, wr) + br
    n_e = logits.shape[1]
    lane = lax.broadcasted_iota(jnp.int32, (m, n_e), 1).astype(F32)
    vals, sels, ohs = [], [], []
    cur = logits
    for _ in range(TOP_K):
        mx = jnp.max(cur, axis=-1, keepdims=True)
        sel = jnp.min(jnp.where(cur == mx, lane, float(n_e)), axis=-1, keepdims=True)
        oh = lane == sel
        vals.append(mx)
        sels.append(sel)
        ohs.append(oh.astype(F32))
        cur = jnp.where(oh, -jnp.inf, cur)
    es = [jnp.exp(v - vals[0]) for v in vals]
    den = es[0] + es[1] + es[2] + es[3]
    gate_ref[...] = _cols4(*[e / den for e in es])
    tope_ref[...] = _cols4(*sels).astype(jnp.int32)
    row = lax.broadcasted_iota(jnp.int32, (m, m), 0)
    col = lax.broadcasted_iota(jnp.int32, (m, m), 1)
    ltri = (row > col).astype(BF16)
    prior = runcnt_ref[...]
    ranks = []
    for oh in ohs:
        within = jnp.dot(ltri, oh.astype(BF16), preferred_element_type=F32)
        ranks.append(jnp.sum((prior + within) * oh, axis=-1, keepdims=True))
        prior = prior + jnp.sum(oh, axis=0, keepdims=True)
    rank_ref[...] = _cols4(*ranks).astype(jnp.int32)
    runcnt_ref[...] = prior
    cnt_ref[...] = prior.astype(jnp.int32)


def _gather_combine(dest_ref, y_hbm, ybuf, sem, gates, tm):
    def issue(t, carry):
        for k in range(TOP_K):
            d = dest_ref[0, 0, t * TOP_K + k]
            pltpu.make_async_copy(y_hbm.at[pl.ds(d, 1)], ybuf.at[k, pl.ds(t, 1)], sem).start()
        return carry
    lax.fori_loop(0, tm, issue, 0)
    for k in range(TOP_K):
        pltpu.make_async_copy(y_hbm.at[pl.ds(0, tm)], ybuf.at[k], sem).wait()
    acc = gates[:, 0:1] * ybuf[0]
    for k in range(1, TOP_K):
        acc = acc + gates[:, k:k + 1] * ybuf[k]
    return acc


def _mixer_a_kernel(*refs, chunked, want_v):
    (x_ref, gmix_ref, win_ref, bin_ref, gv_ref, bv_ref, ws_ref, bs_ref, wout_ref,
     gffn_ref, wr_ref, br_ref) = refs[:12]
    rest = list(refs[12:])
    x1_ref, xn_ref = rest[0], rest[1]
    rest = rest[2:]
    v_ref = rest.pop(0) if want_v else None
    tope_ref, gate_ref, rank_ref, cnt_ref, runcnt_ref = rest[:5]
    us_ref = rest[5] if chunked else None

    @pl.when(pl.program_id(0) == 0)
    def _():
        runcnt_ref[...] = jnp.zeros_like(runcnt_ref)

    x = x_ref[...]
    tm = x.shape[0]
    aw = wout_ref.shape[0]
    h = _rms(x, gmix_ref[...])
    u = _gelu(_mm(h, win_ref[:, :aw]) + bin_ref[:, :aw])
    vr = _gelu(_mm(h, win_ref[:, aw:]) + bin_ref[:, aw:])
    mu = jnp.mean(vr, axis=-1, keepdims=True)
    vc = vr - mu
    v = vc * lax.rsqrt(jnp.mean(vc * vc, axis=-1, keepdims=True) + EPS) * gv_ref[...] + bv_ref[...]
    if want_v:
        v_ref[...] = v
    if chunked:
        n_g, cl = ws_ref.shape[0], ws_ref.shape[1]
        gd = aw // n_g
        tri = (lax.broadcasted_iota(jnp.int32, (cl, cl), 0) >= lax.broadcasted_iota(jnp.int32, (cl, cl), 1))
        for g in range(n_g):
            wm = jnp.where(tri, ws_ref[g], 0.0).astype(BF16)
            for c in range(tm // cl):
                vg = v[c * cl:(c + 1) * cl, g * gd:(g + 1) * gd].astype(BF16)
                s = jnp.dot(wm, vg, preferred_element_type=F32) + bs_ref[g]
                us_ref[c * cl:(c + 1) * cl, g * gd:(g + 1) * gd] = u[c * cl:(c + 1) * cl, g * gd:(g + 1) * gd] * s
        us = us_ref[...]
    else:
        us = u * (v * ws_ref[...] + bs_ref[...])
    x1 = x + _mm(us, wout_ref[...])
    x1_ref[...] = x1
    _router_tail(x1, gffn_ref[...], wr_ref[...], br_ref[...], runcnt_ref,
                 xn_ref, tope_ref, gate_ref, rank_ref, cnt_ref)


def _full(shape):
    nd = len(shape)
    return pl.BlockSpec(shape, lambda *_: (0,) * nd, pipeline_mode=pl.Buffered(1))


def _router_out_specs(t, tm, d, n_e):
    shapes = [jax.ShapeDtypeStruct((t, TOP_K), jnp.int32), jax.ShapeDtypeStruct((t, TOP_K), F32),
              jax.ShapeDtypeStruct((t, TOP_K), jnp.int32), jax.ShapeDtypeStruct((1, n_e), jnp.int32)]
    specs = [pl.BlockSpec((tm, TOP_K), lambda i: (i, 0)), pl.BlockSpec((tm, TOP_K), lambda i: (i, 0)),
             pl.BlockSpec((tm, TOP_K), lambda i: (i, 0)), pl.BlockSpec((1, n_e), lambda i: (0, 0))]
    return shapes, specs


def _mixer_a(x, p, l, *, chunked, want_v, tm):
    t, d = x.shape
    w_in = p['a_w_in'][l].astype(BF16)
    w_out = p['a_w_out'][l].astype(BF16)
    aw = w_out.shape[0]
    n_e = p['moe_w_router'].shape[-1]
    if chunked:
        ws = p['a_w_s'][l]
        bs = p['a_b_s'][l][:, :, None]
    else:
        gd = aw // A_GROUPS
        ws = jnp.repeat(p['a_w_s'][l][:, 0, 0], gd)[None, :]
        bs = jnp.repeat(p['a_b_s'][l][:, 0], gd)[None, :]
    args = [x, p['g_mix'][l][None, :], w_in, p['a_b_in'][l][None, :], p['a_g_v'][l][None, :],
            p['a_b_v'][l][None, :], ws, bs, w_out, p['g_ffn'][l][None, :],
            p['moe_w_router'][l].astype(BF16), p['moe_b_router'][l][None, :]]
    in_specs = [pl.BlockSpec((tm, d), lambda i: (i, 0))] + [_full(a.shape) for a in args[1:]]
    r_shapes, r_specs = _router_out_specs(t, tm, d, n_e)
    out_shape = [jax.ShapeDtypeStruct((t, d), F32), jax.ShapeDtypeStruct((t, d), F32)]
    out_specs = [pl.BlockSpec((tm, d), lambda i: (i, 0)), pl.BlockSpec((tm, d), lambda i: (i, 0))]
    if want_v:
        out_shape.append(jax.ShapeDtypeStruct((t, aw), F32))
        out_specs.append(pl.BlockSpec((tm, aw), lambda i: (i, 0)))
    out_shape += r_shapes
    out_specs += r_specs
    scratch = [pltpu.VMEM((1, n_e), F32)]
    if chunked:
        scratch.append(pltpu.VMEM((tm, aw), F32))
    outs = pl.pallas_call(
        functools.partial(_mixer_a_kernel, chunked=chunked, want_v=want_v),
        grid=(t // tm,), in_specs=in_specs, out_specs=out_specs, out_shape=out_shape,
        scratch_shapes=scratch, compiler_params=_cparams(1), name="mixer_a_router",
    )(*args)
    outs = list(outs)
    x1, xn = outs[0], outs[1]
    v = outs[2] if want_v else None
    tope, gates, rank, cnt = outs[-4:]
    return x1, xn, v, tope, gates, rank, cnt


def _route_plan(routes, tb):
    n_e = routes[0][2].shape[1]
    n_tok = sum(r[0].shape[0] for r in routes)
    counts = sum(r[2][0] for r in routes)
    pc = (counts + tb - 1) // tb * tb
    pend = jnp.cumsum(pc)
    base = pend - pc
    dests = []
    for tope, rank, cnt in routes:
        dests.append((base[tope] + rank).astype(jnp.int32))
        base = base + cnt[0]
    n_blocks = -(-(n_tok * TOP_K) // tb) + n_e
    blk_start = jnp.arange(n_blocks, dtype=jnp.int32) * tb
    blk_e = jnp.minimum(jnp.sum(pend[None, :] <= blk_start[:, None], axis=1), n_e - 1).astype(jnp.int32)
    n_used = (pend[-1] // tb).astype(jnp.int32)[None]
    return dests, blk_e, n_used, n_blocks


def _dispatch_kernel(dest_ref, xn_ref, xs_in_hbm, xs_hbm, sem, *, td):
    del xs_in_hbm

    def issue(t, carry):
        for k in range(TOP_K):
            d = dest_ref[0, 0, t * TOP_K + k]
            pltpu.make_async_copy(xn_ref.at[pl.ds(t, 1)], xs_hbm.at[pl.ds(d, 1)], sem).start()
        return carry
    lax.fori_loop(0, td, issue, 0)
    for _ in range(TOP_K):
        pltpu.make_async_copy(xn_ref, xs_hbm.at[pl.ds(0, td)], sem).wait()


def _dispatch(xn, dest, rows, td):
    t, d = xn.shape
    dest3 = dest.reshape(t // td, 1, td * TOP_K)
    return pl.pallas_call(
        functools.partial(_dispatch_kernel, td=td),
        grid=(t // td,),
        in_specs=[pl.BlockSpec((1, 1, td * TOP_K), lambda i: (i, 0, 0), memory_space=pltpu.SMEM),
                  pl.BlockSpec((td, d), lambda i: (i, 0)), pl.BlockSpec(memory_space=pl.ANY)],
        out_specs=pl.BlockSpec(memory_space=pl.ANY),
        out_shape=jax.ShapeDtypeStruct(rows.shape, F32),
        scratch_shapes=[pltpu.SemaphoreType.DMA(())],
        input_output_aliases={2: 0},
        compiler_params=_cparams(1), name="moe_dispatch",
    )(dest3, xn, rows)


def _experts_kernel(be_ref, nu_ref, xs_ref, wgu_ref, bgu_ref, wdn_ref, bdn_ref, y_ref, wgu_bf, wdn_bf):
    i = pl.program_id(0)
    live = i < nu_ref[0]
    f = wdn_ref.shape[2]
    changed = jnp.logical_or(i == 0, be_ref[i] != be_ref[jnp.maximum(i - 1, 0)])

    @pl.when(jnp.logical_and(live, changed))
    def _():
        wgu_bf[...] = wgu_ref[0, 0].astype(BF16)
        wdn_bf[...] = wdn_ref[0, 0].astype(BF16)

    @pl.when(live)
    def _():
        x = xs_ref[...]
        hgu = _mm(x, wgu_bf[...]) + bgu_ref[0, 0]
        gate = jnp.minimum(hgu[:, :f], SWIGLU_LIMIT)
        up = jnp.clip(hgu[:, f:], -SWIGLU_LIMIT, SWIGLU_LIMIT)
        act = gate * jax.nn.sigmoid(SWIGLU_ALPHA * gate) * (up + 1.0)
        y_ref[...] = _mm(act, wdn_bf[...]) + bdn_ref[0, 0]

    @pl.when(jnp.logical_not(live))
    def _():
        y_ref[...] = jnp.zeros_like(y_ref)


def _experts(xs, blk_e, n_used, n_blocks, p, l, tb):
    n_rows, d = xs.shape
    wgu, wdn = p['moe_w_gu'], p['moe_w_down']
    _, n_e, _, f2 = wgu.shape
    f = f2 // 2
    bgu = p['moe_b_gu'][:, :, None, :]
    bdn = p['moe_b_down'][:, :, None, :]

    def row_map(i, be, nu):
        return (jnp.minimum(i, nu[0] - 1), 0)

    def w_map(i, be, nu):
        return (l, be[jnp.minimum(i, nu[0] - 1)], 0, 0)

    return pl.pallas_call(
        _experts_kernel,
        grid_spec=pltpu.PrefetchScalarGridSpec(
            num_scalar_prefetch=2, grid=(n_blocks,),
            in_specs=[pl.BlockSpec((tb, d), row_map), pl.BlockSpec((1, 1, d, f2), w_map),
                      pl.BlockSpec((1, 1, 1, f2), w_map), pl.BlockSpec((1, 1, f, d), w_map),
                      pl.BlockSpec((1, 1, 1, d), w_map)],
            out_specs=pl.BlockSpec((tb, d), lambda i, be, nu: (i, 0)),
            scratch_shapes=[pltpu.VMEM((d, f2), BF16), pltpu.VMEM((f, d), BF16)]),
        out_shape=jax.ShapeDtypeStruct((n_rows, d), F32),
        compiler_params=_cparams(1), name="moe_experts",
    )(blk_e, n_used, xs, wgu, bgu, wdn, bdn)


def _moe_rows(groups, p, l, spare_rows=None):
    tb = PROMPT_EXPERT_BLOCK
    dests, blk_e, n_used, n_blocks = _route_plan([g[1:4] for g in groups], tb)
    xs = jnp.zeros((n_blocks * tb, groups[0][0].shape[1]), F32) if spare_rows is None else spare_rows
    for (xn, _, _, _, td), dest in zip(groups, dests):
        xs = _dispatch(xn, dest, xs, td)
    return _experts(xs, blk_e, n_used, n_blocks, p, l, tb), dests, xs


def _combine_kvq_kernel(dest_ref, x1_ref, gate_ref, y_hbm, gkv_ref, wk_ref, wv_ref, gmix_ref, wq_ref,
                        x2_ref, k_ref, v_ref, *rest, for_attn):
    if for_attn:
        qb_ref, kb_ref, vt_ref, ybuf, sem = rest
    else:
        q_ref, ybuf, sem = rest
    tm = x1_ref.shape[0]
    x2 = x1_ref[...] + _gather_combine(dest_ref, y_hbm, ybuf, sem, gate_ref[...], tm)
    x2_ref[...] = x2
    hk = _rms(x2, gkv_ref[...])
    k = _mm(hk, wk_ref[...])
    v = _mm(hk, wv_ref[...])
    k_ref[...] = k
    v_ref[...] = v
    q = _mm(_rms(x2, gmix_ref[...]), wq_ref[...])
    if for_attn:
        qb_ref[...] = (q * (ATTN_SCALE * LOG2E)).astype(BF16)
        kb_ref[...] = k.astype(BF16)
        n_h = vt_ref.shape[0]
        vt_ref[:, 0, :V_DIM] = v.T.reshape(n_h, V_DIM, tm).astype(BF16)
        ones_row = lax.broadcasted_iota(jnp.int32, (n_h, VT_PAD_ROWS, tm), 1) == 0
        vt_ref[:, 0, V_DIM:] = jnp.where(ones_row, 1.0, 0.0).astype(BF16)
    else:
        q_ref[...] = q


def _combine_kvq(x1, gates, y_rows, dest, p, *, for_attn, tm):
    t, d = x1.shape
    dest3 = dest.reshape(t // tm, 1, tm * TOP_K)
    args = [dest3, x1, gates, y_rows, p['g_kv'][None, :], p['w_k'].astype(BF16), p['w_v'].astype(BF16),
            p['g_mix'][N_A_LAYERS][None, :], p['b_w_q'][0].astype(BF16)]
    tile = pl.BlockSpec((tm, d), lambda i: (i, 0))
    in_specs = [pl.BlockSpec((1, 1, tm * TOP_K), lambda i: (i, 0, 0), memory_space=pltpu.SMEM),
                tile, pl.BlockSpec((tm, TOP_K), lambda i: (i, 0)), pl.BlockSpec(memory_space=pl.ANY)]
    in_specs += [_full(a.shape) for a in args[4:]]
    out_shape = [jax.ShapeDtypeStruct((t, d), F32)] * 3
    out_specs = [tile] * 3
    if for_attn:
        n_h = d // V7X_LANES
        out_shape += [jax.ShapeDtypeStruct((t, d), BF16), jax.ShapeDtypeStruct((t, d), BF16),
                      jax.ShapeDtypeStruct((n_h, t // tm, V_DIM + VT_PAD_ROWS, tm), BF16)]
        out_specs += [tile, tile, pl.BlockSpec((n_h, 1, V_DIM + VT_PAD_ROWS, tm), lambda i: (0, i, 0, 0))]
    else:
        out_shape += [jax.ShapeDtypeStruct((t, d), F32)]
        out_specs += [tile]
    return pl.pallas_call(
        functools.partial(_combine_kvq_kernel, for_attn=for_attn),
        grid=(t // tm,), in_specs=in_specs, out_specs=out_specs, out_shape=out_shape,
        scratch_shapes=[pltpu.VMEM((TOP_K, tm, d), F32), pltpu.SemaphoreType.DMA(())],
        compiler_params=_cparams(1), name="moe_combine_kvq",
    )(*args)


def _bucket_np(n):
    n = np.asarray(n)
    nf = np.maximum(n, 1).astype(np.float32)
    large = MAX_EXACT + (np.log(nf / np.float32(MAX_EXACT)) / np.float32(math.log(MAX_DISTANCE / MAX_EXACT))
                         * np.float32(N_BUCKETS - MAX_EXACT)).astype(np.int32)
    return np.where(n < MAX_EXACT, n, np.minimum(large, N_BUCKETS - 1)).astype(np.int32)


def _bias_tiles_kernel(tbl_ref, bkt_ref, bkt_s_ref, tile_ref, last_ref, zero_ref):
    h = pl.program_id(0)
    far = tbl_ref[N_BUCKETS - 1, h]
    for dl in range(2):
        b = bkt_ref[dl]
        acc = jnp.where(b < 0, NEG_BIG, 0.0).astype(F32)
        for bb in range(N_BUCKETS - 1):
            acc = jnp.where(b == bb, (tbl_ref[bb, h] - far) * LOG2E, acc)
        tile_ref[0, dl] = acc

    @pl.when(h == 0)
    def _():
        last_ref[...] = jnp.zeros_like(last_ref)
        zero_ref[...] = jnp.zeros_like(zero_ref)

    bs = bkt_s_ref[...]
    val = jnp.zeros(bs.shape, F32)
    for bb in range(N_BUCKETS - 1):
        val = jnp.where(bs == bb, tbl_ref[bb, h] - far, val)
    row = lax.broadcasted_iota(jnp.int32, last_ref.shape, 0) % N_HEADS
    last_ref[...] = jnp.where(row == h, val, last_ref[...])
    zero_ref[...] = jnp.where(row == h, tbl_ref[0, h] - far, zero_ref[...])


def _bias_tiles(rel_table, tb, page):
    assert tb >= MAX_DISTANCE and page >= MAX_DISTANCE and page == V7X_LANES
    kq = np.arange(tb)
    tiles = []
    for dl in range(2):
        dist = dl * tb + kq[None, :] - kq[:, None]
        tiles.append(np.where(dist >= 0, _bucket_np(np.maximum(dist, 0)), -1))
    bkt = jnp.asarray(np.stack(tiles).astype(np.int32))
    bkt_s = jnp.asarray(_bucket_np(page - np.arange(page))[None, :].astype(np.int32))
    rows = (2 * N_HEADS, V7X_LANES)
    return pl.pallas_call(
        _bias_tiles_kernel,
        grid=(N_HEADS,),
        in_specs=[pl.BlockSpec(memory_space=pltpu.SMEM), _full(bkt.shape), _full(bkt_s.shape)],
        out_specs=[pl.BlockSpec((1, 2, tb, tb), lambda h: (h, 0, 0, 0)),
                   pl.BlockSpec(rows, lambda h: (0, 0)), pl.BlockSpec(rows, lambda h: (0, 0))],
        out_shape=[jax.ShapeDtypeStruct((N_HEADS, 2, tb, tb), F32),
                   jax.ShapeDtypeStruct(rows, F32), jax.ShapeDtypeStruct(rows, F32)],
        compiler_params=_cparams(1), name="rel_bias_tiles",
    )(rel_table, bkt, bkt_s)


def _attn_prompt_kernel(lam_ref, q_ref, k_ref, vt_ref, bias_ref, o_ref, acc_ref, m_ref, sa_ref, sb_ref,
                        pa_ref, pb_ref, *, tb):
    qi = pl.program_id(1)
    q = q_ref[...]
    lane = lax.broadcasted_iota(jnp.int32, q.shape, 1)
    zero = jnp.zeros_like(q)
    qcat = jnp.concatenate([jnp.where(lane < HEAD_DIM, q, zero), jnp.where(lane >= HEAD_DIM, q, zero)], axis=0)
    acc_ref[...] = jnp.zeros_like(acc_ref)

    def scores(j, bias=None):
        k_blk = k_ref[pl.ds(pl.multiple_of(j * tb, tb), tb), :]
        s = lax.dot_general(k_blk, qcat, (((1,), (1,)), ((), ())), preferred_element_type=F32)
        if bias is not None:
            s = s + jnp.concatenate([bias, bias], axis=1)
        return s, jnp.max(s, axis=0, keepdims=True)

    def absorb(s_ref, p_ref, s_max, j, m_old):
        m_new = jnp.maximum(m_old, s_max)
        for c in range(0, tb, ATTN_ROW_CHUNK):
            p_ref[c:c + ATTN_ROW_CHUNK, :] = jnp.exp2((s_ref[c:c + ATTN_ROW_CHUNK, :] - m_new).astype(BF16))
        acc_ref[...] = (jnp.exp2(m_old - m_new) * acc_ref[...]
                        + jnp.dot(vt_ref[0, j], p_ref[...], preferred_element_type=F32))
        return m_new

    n_far = jnp.maximum(qi - 1, 0)
    n_pairs = lax.shift_right_logical(n_far, 1)
    neg = jnp.full((1, 2 * tb), NEG_BIG, F32)
    m_ref[...] = neg

    @pl.when(n_pairs > 0)
    def _():
        sa_ref[...], m_ref[...] = scores(0)

    def pair(i, carry):
        m_old, max_a = carry
        a = 2 * i
        sb_ref[...], max_b = scores(a + 1)
        m_old = absorb(sa_ref, pa_ref, max_a, a, m_old)
        sa_ref[...], max_a = scores(a + 2)
        m_old = absorb(sb_ref, pb_ref, max_b, a + 1, m_old)
        return m_old, max_a

    m_ref[...], _ = lax.fori_loop(0, n_pairs, pair, (neg, m_ref[...]))

    def single(j, bias=None):
        sa_ref[...], s_max = scores(j, bias)
        m_ref[...] = absorb(sa_ref, pa_ref, s_max, j, m_ref[...])

    @pl.when(n_far > 2 * n_pairs)
    def _():
        single(2 * n_pairs)

    @pl.when(qi >= 1)
    def _():
        single(qi - 1, bias_ref[0, 1])
    single(qi, bias_ref[0, 0])
    o_all = acc_ref[:V_DIM, :] / acc_ref[V_DIM:V_DIM + 1, :]
    o_ref[...] = (o_all[:, :tb] - lam_ref[0] * o_all[:, tb:]).T


def _attn_prompt(qb, kb, vt4, bias_tiles, lam, tb):
    t, d = qb.shape
    n_h = d // V7X_LANES
    n_q = t // tb
    vt_rows = vt4.shape[2]
    return pl.pallas_call(
        functools.partial(_attn_prompt_kernel, tb=tb),
        grid=(n_h, n_q),
        in_specs=[pl.BlockSpec(memory_space=pltpu.SMEM),
                  pl.BlockSpec((tb, V7X_LANES), lambda h, i: (i, h)),
                  pl.BlockSpec((t, V7X_LANES), lambda h, i: (0, h)),
                  pl.BlockSpec((1, n_q, vt_rows, tb), lambda h, i: (h, 0, 0, 0)),
                  pl.BlockSpec((1, 2, tb, tb), lambda h, i: (h, 0, 0, 0))],
        out_specs=pl.BlockSpec((tb, V7X_LANES), lambda h, i: (i, h)),
        out_shape=jax.ShapeDtypeStruct((t, d), F32),
        scratch_shapes=[pltpu.VMEM((vt_rows, 2 * tb), F32), pltpu.VMEM((1, 2 * tb), F32),
                        pltpu.VMEM((tb, 2 * tb), F32), pltpu.VMEM((tb, 2 * tb), F32),
                        pltpu.VMEM((tb, 2 * tb), BF16), pltpu.VMEM((tb, 2 * tb), BF16)],
        compiler_params=_cparams(2), name="attn_prompt",
    )(lam, qb, kb, vt4, bias_tiles)


def _attn_sample_kernel(pt_ref, lam_ref, qbd_ref, q_ref, kn_ref, vn_ref, hmask_ref, expand_ref, blast_ref,
                        bzero_ref, *rest, n_pp):
    k_refs = rest[:n_pp]
    v_refs = rest[n_pp:2 * n_pp]
    o_ref, acc_ref, m_ref, l_ref = rest[2 * n_pp:]
    del pt_ref
    g = pl.program_id(1)
    n_g = pl.num_programs(1)
    n_h = N_HEADS

    @pl.when(g == 0)
    def _():
        qv = q_ref[0].astype(BF16).astype(F32)
        s0 = jnp.sum(qv * kn_ref[0].astype(BF16).astype(F32), axis=-1, keepdims=True)
        m_ref[...] = s0 + bzero_ref[:, 0:1]
        l_ref[...] = jnp.ones_like(l_ref)
        vn = vn_ref[0].astype(BF16).astype(F32)
        acc_ref[...] = jnp.concatenate([vn, vn], axis=0)

    qbd = qbd_ref[0].astype(BF16)
    parts = []
    for i in range(n_pp):
        kst = k_refs[i][0]
        kst = kst.reshape(kst.shape[0] * kst.shape[1], kst.shape[2]).astype(BF16)
        parts.append(jnp.dot(qbd, kst, preferred_element_type=F32))
    parts[-1] = parts[-1] + jnp.where(g == n_g - 1, 1.0, 0.0) * blast_ref[...]
    s = jnp.concatenate(parts, axis=1)
    m_old = m_ref[...]
    m_new = jnp.maximum(m_old, jnp.max(s, axis=-1, keepdims=True))
    alpha = jnp.exp(m_old - m_new)
    pr = jnp.exp(s - m_new)
    l_ref[...] = alpha * l_ref[...] + jnp.sum(pr, axis=-1, keepdims=True)
    m_ref[...] = m_new
    keys = pr.shape[1] // n_pp
    n_m = pr.shape[0]
    p_rows = jnp.concatenate([pr[:, i * keys:(i + 1) * keys] for i in range(n_pp)], axis=0).astype(BF16)
    spread = jnp.dot(p_rows, expand_ref[...], preferred_element_type=F32) * hmask_ref[...]
    spread = spread.astype(BF16)
    acc = alpha * acc_ref[...]
    for i in range(n_pp):
        acc = acc + jnp.dot(spread[i * n_m:(i + 1) * n_m], v_refs[i][0].astype(BF16),
                            preferred_element_type=F32)
    acc_ref[...] = acc

    @pl.when(g == n_g - 1)
    def _():
        o = acc_ref[...] / l_ref[...]
        o_ref[0] = o[:n_h] - lam_ref[0] * o[n_h:]


def _attn_sample(q, k_new, v_new, cache_k, cache_v, page_table, blast, bzero, lam):
    nb, d = q.shape
    n_phys, page = cache_k.shape[0], cache_k.shape[1]
    n_pages = page_table.shape[1]
    n_pp = math.gcd(PAGES_PER_STEP, n_pages)
    n_h = N_HEADS
    n_m = 2 * n_h
    perm = np.concatenate([np.arange(0, n_m, 2), np.arange(1, n_m, 2)])
    qp = (q * ATTN_SCALE).reshape(nb, n_m, HEAD_DIM)[:, perm]
    kn = k_new.reshape(nb, n_m, HEAD_DIM)[:, perm]
    vn = v_new.reshape(nb, n_h, V_DIM)
    own = jnp.asarray((perm[:, None] == np.arange(n_m)[None, :])[None, :, :, None])
    qbd = jnp.where(own, qp[:, :, None, :], 0.0).reshape(nb, n_m, n_m * HEAD_DIM)
    ckt = jnp.transpose(cache_k, (0, 2, 3, 1))
    cv = cache_v.reshape(n_phys, page * n_h, V_DIM)
    lanes = np.arange(page * n_h)
    hmask = jnp.asarray((lanes[None, :] % n_h == np.arange(n_pp * n_m)[:, None] % n_h).astype(np.float32))
    expand = jnp.asarray((lanes[None, :] // n_h == np.arange(page)[:, None]).astype(np.float32), dtype=BF16)

    def page_map(i):
        return lambda b, g, pt: (pt[b * n_pages + g * n_pp + i], 0, 0, 0)

    def page_map3(i):
        return lambda b, g, pt: (pt[b * n_pages + g * n_pp + i], 0, 0)
    row3 = lambda b, g, pt: (b, 0, 0)
    const2 = lambda b, g, pt: (0, 0)
    in_specs = [pl.BlockSpec(memory_space=pltpu.SMEM),
                pl.BlockSpec((1, n_m, n_m * HEAD_DIM), row3),
                pl.BlockSpec((1, n_m, HEAD_DIM), row3), pl.BlockSpec((1, n_m, HEAD_DIM), row3),
                pl.BlockSpec((1, n_h, V_DIM), row3), pl.BlockSpec(hmask.shape, const2),
                pl.BlockSpec(expand.shape, const2), pl.BlockSpec(blast.shape, const2),
                pl.BlockSpec(bzero.shape, const2)]
    in_specs += [pl.BlockSpec((1, n_m, HEAD_DIM, page), page_map(i)) for i in range(n_pp)]
    in_specs += [pl.BlockSpec((1, page * n_h, V_DIM), page_map3(i)) for i in range(n_pp)]
    out = pl.pallas_call(
        functools.partial(_attn_sample_kernel, n_pp=n_pp),
        grid_spec=pltpu.PrefetchScalarGridSpec(
            num_scalar_prefetch=1, grid=(nb, n_pages // n_pp), in_specs=in_specs,
            out_specs=pl.BlockSpec((1, n_h, V_DIM), row3),
            scratch_shapes=[pltpu.VMEM((n_m, V_DIM), F32), pltpu.VMEM((n_m, 1), F32),
                            pltpu.VMEM((n_m, 1), F32)]),
        out_shape=jax.ShapeDtypeStruct((nb, n_h, V_DIM), F32),
        compiler_params=_cparams(2), name="attn_sample",
    )(page_table.reshape(-1).astype(jnp.int32), lam, qbd, qp, kn, vn, hmask, expand, blast, bzero,
      *([ckt] * n_pp), *([cv] * n_pp))
    return out.reshape(nb, d)


def _attn_out_kernel(o_ref, x2_ref, gsub_ref, wo_ref, gffn_ref, wr_ref, br_ref,
                     x3_ref, xn_ref, tope_ref, gate_ref, rank_ref, cnt_ref, runcnt_ref, on_ref,
                     *, sub_scale):
    @pl.when(pl.program_id(0) == 0)
    def _():
        runcnt_ref[...] = jnp.zeros_like(runcnt_ref)

    vd = gsub_ref.shape[1]
    for h in range(o_ref.shape[1] // vd):
        oh = o_ref[:, h * vd:(h + 1) * vd]
        on_ref[:, h * vd:(h + 1) * vd] = _rms(oh, gsub_ref[...]) * sub_scale
    x3 = x2_ref[...] + _mm(on_ref[...], wo_ref[...])
    x3_ref[...] = x3
    _router_tail(x3, gffn_ref[...], wr_ref[...], br_ref[...], runcnt_ref,
                 xn_ref, tope_ref, gate_ref, rank_ref, cnt_ref)


def _attn_out(o, x2, p, l, sub_scale, *, tm):
    t, d = x2.shape
    n_e = p['moe_w_router'].shape[-1]
    args = [o, x2, p['b_g_sub'][0][None, :], p['b_w_o'][0].astype(BF16), p['g_ffn'][l][None, :],
            p['moe_w_router'][l].astype(BF16), p['moe_b_router'][l][None, :]]
    tile = pl.BlockSpec((tm, d), lambda i: (i, 0))
    r_shapes, r_specs = _router_out_specs(t, tm, d, n_e)
    outs = pl.pallas_call(
        functools.partial(_attn_out_kernel, sub_scale=sub_scale),
        grid=(t // tm,), in_specs=[tile, tile] + [_full(a.shape) for a in args[2:]],
        out_specs=[tile, tile] + r_specs,
        out_shape=[jax.ShapeDtypeStruct((t, d), F32)] * 2 + r_shapes,
        scratch_shapes=[pltpu.VMEM((1, n_e), F32), pltpu.VMEM((tm, d), F32)],
        compiler_params=_cparams(1), name="attn_out_router",
    )(*args)
    return outs


def _combine_final_kernel(dest_ref, x3_ref, gate_ref, y_hbm, gout_ref, y_ref, ybuf, sem):
    tm = x3_ref.shape[0]
    x4 = x3_ref[...] + _gather_combine(dest_ref, y_hbm, ybuf, sem, gate_ref[...], tm)
    y_ref[...] = _rms(x4, gout_ref[...])


def _combine_final(x3, gates, y_rows, dest, g_out, tm):
    t, d = x3.shape
    dest3 = dest.reshape(t // tm, 1, tm * TOP_K)
    tile = pl.BlockSpec((tm, d), lambda i: (i, 0))
    return pl.pallas_call(
        _combine_final_kernel,
        grid=(t // tm,),
        in_specs=[pl.BlockSpec((1, 1, tm * TOP_K), lambda i: (i, 0, 0), memory_space=pltpu.SMEM),
                  tile, pl.BlockSpec((tm, TOP_K), lambda i: (i, 0)), pl.BlockSpec(memory_space=pl.ANY),
                  _full((1, d))],
        out_specs=tile, out_shape=jax.ShapeDtypeStruct((t, d), F32),
        scratch_shapes=[pltpu.VMEM((TOP_K, tm, d), F32), pltpu.SemaphoreType.DMA(())],
        compiler_params=_cparams(1), name="moe_combine_final",
    )(dest3, x3, gates, y_rows, g_out[None, :])


def kernel(x_prompt, x_sample, cache_k, cache_v, page_table, g_mix, g_ffn, g_kv, g_out, a_w_in, a_b_in, a_g_v, a_b_v, a_w_s, a_b_s, a_w_out, w_k, w_v, b_w_q, b_lq1, b_lk1, b_lq2, b_lk2, b_g_sub, b_w_o, rel_table, moe_w_router, moe_b_router, moe_w_gu, moe_b_gu, moe_w_down, moe_b_down):
    p = {
        'g_mix': g_mix, 'g_ffn': g_ffn, 'g_kv': g_kv, 'g_out': g_out,
        'a_w_in': a_w_in, 'a_b_in': a_b_in, 'a_g_v': a_g_v, 'a_b_v': a_b_v,
        'a_w_s': a_w_s, 'a_b_s': a_b_s, 'a_w_out': a_w_out,
        'w_k': w_k, 'w_v': w_v, 'b_w_q': b_w_q, 'b_g_sub': b_g_sub, 'b_w_o': b_w_o,
        'moe_w_router': moe_w_router, 'moe_b_router': moe_b_router,
        'moe_w_gu': moe_w_gu, 'moe_b_gu': moe_b_gu, 'moe_w_down': moe_w_down, 'moe_b_down': moe_b_down,
    }
    assert x_prompt.shape[0] == 1 and x_sample.shape[1] == 1
    b, s, d = x_prompt.shape
    nb = x_sample.shape[0]
    page = cache_k.shape[1]
    lam_init = 0.8 - 0.6 * math.exp(-0.3 * N_A_LAYERS)
    lam = (jnp.exp(jnp.sum(b_lq1[0] * b_lk1[0])) - jnp.exp(jnp.sum(b_lq2[0] * b_lk2[0])) + lam_init)
    lam = lam.astype(F32)[None]
    tb = min(ATTN_BLOCK, s)
    bias_tiles, blast, bzero = _bias_tiles(rel_table, tb, page)
    tm_p, tm_s = PROMPT_TOKEN_TILE, nb
    assert s % tm_p == 0 and tm_p % CHUNK == 0 and s % tb == 0

    x1_p, xn_p, _, tope_p, gates_p, rank_p, cnt_p = _mixer_a(
        x_prompt[0], p, 0, chunked=True, want_v=False, tm=tm_p)
    x1_s, xn_s, cv_s, tope_s, gates_s, rank_s, cnt_s = _mixer_a(
        x_sample[:, 0], p, 0, chunked=False, want_v=True, tm=tm_s)
    y_rows, (dest_p, dest_s), rows_buf = _moe_rows(
        [(xn_p, tope_p, rank_p, cnt_p, tm_p), (xn_s, tope_s, rank_s, cnt_s, tm_s)], p, 0)
    x2_p, k_p, v_p, qb, kb, vt4 = _combine_kvq(x1_p, gates_p, y_rows, dest_p, p, for_attn=True, tm=tb)
    x2_s, k_s, v_s, q_s = _combine_kvq(x1_s, gates_s, y_rows, dest_s, p, for_attn=False, tm=tm_s)

    o_p = _attn_prompt(qb, kb, vt4, bias_tiles, lam, tb)
    o_s = _attn_sample(q_s, k_s, v_s, cache_k, cache_v, page_table, blast, bzero, lam)
    x3_p, xn_p, tope_p, gates_p, rank_p, cnt_p = _attn_out(o_p, x2_p, p, 1, 1.0 - lam_init, tm=tm_p)
    x3_s, xn_s, tope_s, gates_s, rank_s, cnt_s = _attn_out(o_s, x2_s, p, 1, 1.0 - lam_init, tm=tm_s)
    y_rows, (dest_p, dest_s), _ = _moe_rows(
        [(xn_p, tope_p, rank_p, cnt_p, tm_p), (xn_s, tope_s, rank_s, cnt_s, tm_s)], p, 1, rows_buf)
    y_p = _combine_final(x3_p, gates_p, y_rows, dest_p, g_out, tm_p)
    y_s = _combine_final(x3_s, gates_s, y_rows, dest_s, g_out, tm_s)
    n_h = N_HEADS
    return (y_p[None], y_s[:, None],
            k_p.reshape(1, s, 2 * n_h, HEAD_DIM), v_p.reshape(1, s, n_h, V_DIM),
            k_s.reshape(nb, 1, 2 * n_h, HEAD_DIM), v_s.reshape(nb, 1, n_h, V_DIM),
            cv_s.reshape(1, nb, 1, -1))
```

```python
import functools
import math

import numpy as np
import jax
import jax.numpy as jnp
from jax import lax
from jax.experimental import pallas as pl
from jax.experimental.pallas import tpu as pltpu

F32 = jnp.float32
BF16 = jnp.bfloat16

EPS = 1e-6
CHUNK = 128
A_GROUPS = 8
N_HEADS = 8
HEAD_DIM = 64
V_DIM = 2 * HEAD_DIM
ATTN_SCALE = HEAD_DIM ** -0.5
N_BUCKETS = 32
MAX_EXACT = N_BUCKETS // 2
MAX_DISTANCE = 128
TOP_K = 4
SWIGLU_LIMIT = 7.0
SWIGLU_ALPHA = 1.702
N_A_LAYERS = 1

V7X_LANES = 128
V7X_SUBLANES = 8
V7X_VMEM_LIMIT_BYTES = 56 * 1024 * 1024

NEG_BIG = -1e30
PROMPT_TOKEN_TILE = 256
PROMPT_EXPERT_BLOCK = 512
ATTN_BLOCK = 512
ATTN_ROW_CHUNK = 64
VT_PAD_ROWS = 16
LOG2E = math.log2(math.e)
PAGES_PER_STEP = 8


def _cparams(n_axes):
    return pltpu.CompilerParams(
        dimension_semantics=("arbitrary",) * n_axes,
        vmem_limit_bytes=V7X_VMEM_LIMIT_BYTES,
    )


def _mm(a, w):
    return jnp.dot(a.astype(BF16), w.astype(BF16), preferred_element_type=F32)


def _rms(x, g):
    return x * lax.rsqrt(jnp.mean(x * x, axis=-1, keepdims=True) + EPS) * g


def _gelu(x):
    return 0.5 * x * (1.0 + lax.erf(x * (2.0 ** -0.5)))


def _cols4(c0, c1, c2, c3):
    m = c0.shape[0]
    lane = lax.broadcasted_iota(jnp.int32, (m, TOP_K), 1)
    return jnp.where(lane == 0, c0, jnp.where(lane == 1, c1, jnp.where(lane == 2, c2, c3)))


def _router_tail(x1, gffn, wr, br, runcnt_ref, xn_ref, tope_ref, gate_ref, rank_ref, cnt_ref):
    m = x1.shape[0]
    xn = _rms(x1, gffn)
    xn_ref[...] = xn
    logits = _mm(xn, wr) + br
    n_e = logits.shape[1]
    lane = lax.broadcasted_iota(jnp.int32, (m, n_e), 1).astype(F32)
    vals, sels, ohs = [], [], []
    cur = logits
    for _ in range(TOP_K):
        mx = jnp.max(cur, axis=-1, keepdims=True)
        sel = jnp.min(jnp.where(cur == mx, lane, float(n_e)), axis=-1, keepdims=True)
        oh = lane == sel
        vals.append(mx)
        sels.append(sel)
        ohs.append(oh.astype(F32))
        cur = jnp.where(oh, -jnp.inf, cur)
    es = [jnp.exp(v - vals[0]) for v in vals]
    den = es[0] + es[1] + es[2] + es[3]
    gate_ref[...] = _cols4(*[e / den for e in es])
    tope_ref[...] = _cols4(*sels).astype(jnp.int32)
    row = lax.broadcasted_iota(jnp.int32, (m, m), 0)
    col = lax.broadcasted_iota(jnp.int32, (m, m), 1)
    ltri = (row > col).astype(BF16)
    prior = runcnt_ref[...]
    ranks = []
    for oh in ohs:
        within = jnp.dot(ltri, oh.astype(BF16), preferred_element_type=F32)
        ranks.append(jnp.sum((prior + within) * oh, axis=-1, keepdims=True))
        prior = prior + jnp.sum(oh, axis=0, keepdims=True)
    rank_ref[...] = _cols4(*ranks).astype(jnp.int32)
    runcnt_ref[...] = prior
    cnt_ref[...] = prior.astype(jnp.int32)


def _gather_combine(dest_ref, y_hbm, ybuf, sem, gates, tm):
    def issue(t, carry):
        for k in range(TOP_K):
            d = dest_ref[0, 0, t * TOP_K + k]
            pltpu.make_async_copy(y_hbm.at[pl.ds(d, 1)], ybuf.at[k, pl.ds(t, 1)], sem).start(priority=k % 2)
        return carry
    lax.fori_loop(0, tm, issue, 0)
    for k in range(TOP_K):
        pltpu.make_async_copy(y_hbm.at[pl.ds(0, tm)], ybuf.at[k], sem).wait()
    acc = gates[:, 0:1] * ybuf[0]
    for k in range(1, TOP_K):
        acc = acc + gates[:, k:k + 1] * ybuf[k]
    return acc


def _mixer_a_kernel(*refs, chunked, want_v):
    (x_ref, gmix_ref, win_ref, bin_ref, gv_ref, bv_ref, ws_ref, bs_ref, wout_ref,
     gffn_ref, wr_ref, br_ref) = refs[:12]
    rest = list(refs[12:])
    x1_ref, xn_ref = rest[0], rest[1]
    rest = rest[2:]
    v_ref = rest.pop(0) if want_v else None
    tope_ref, gate_ref, rank_ref, cnt_ref, runcnt_ref = rest[:5]
    us_ref = rest[5] if chunked else None

    @pl.when(pl.program_id(0) == 0)
    def _():
        runcnt_ref[...] = jnp.zeros_like(runcnt_ref)

    x = x_ref[...]
    tm = x.shape[0]
    aw = wout_ref.shape[0]
    h = _rms(x, gmix_ref[...])
    u = _gelu(_mm(h, win_ref[:, :aw]) + bin_ref[:, :aw])
    vr = _gelu(_mm(h, win_ref[:, aw:]) + bin_ref[:, aw:])
    mu = jnp.mean(vr, axis=-1, keepdims=True)
    vc = vr - mu
    v = vc * lax.rsqrt(jnp.mean(vc * vc, axis=-1, keepdims=True) + EPS) * gv_ref[...] + bv_ref[...]
    if want_v:
        v_ref[...] = v
    if chunked:
        n_g, cl = ws_ref.shape[0], ws_ref.shape[1]
        gd = aw // n_g
        tri = (lax.broadcasted_iota(jnp.int32, (cl, cl), 0) >= lax.broadcasted_iota(jnp.int32, (cl, cl), 1))
        for g in range(n_g):
            wm = jnp.where(tri, ws_ref[g], 0.0).astype(BF16)
            for c in range(tm // cl):
                vg = v[c * cl:(c + 1) * cl, g * gd:(g + 1) * gd].astype(BF16)
                s = jnp.dot(wm, vg, preferred_element_type=F32) + bs_ref[g]
                us_ref[c * cl:(c + 1) * cl, g * gd:(g + 1) * gd] = u[c * cl:(c + 1) * cl, g * gd:(g + 1) * gd] * s
        us = us_ref[...]
    else:
        us = u * (v * ws_ref[...] + bs_ref[...])
    x1 = x + _mm(us, wout_ref[...])
    x1_ref[...] = x1
    _router_tail(x1, gffn_ref[...], wr_ref[...], br_ref[...], runcnt_ref,
                 xn_ref, tope_ref, gate_ref, rank_ref, cnt_ref)


def _full(shape):
    nd = len(shape)
    return pl.BlockSpec(shape, lambda *_: (0,) * nd, pipeline_mode=pl.Buffered(1))


def _router_out_specs(t, tm, d, n_e):
    shapes = [jax.ShapeDtypeStruct((t, TOP_K), jnp.int32), jax.ShapeDtypeStruct((t, TOP_K), F32),
              jax.ShapeDtypeStruct((t, TOP_K), jnp.int32), jax.ShapeDtypeStruct((1, n_e), jnp.int32)]
    specs = [pl.BlockSpec((tm, TOP_K), lambda i: (i, 0)), pl.BlockSpec((tm, TOP_K), lambda i: (i, 0)),
             pl.BlockSpec((tm, TOP_K), lambda i: (i, 0)), pl.BlockSpec((1, n_e), lambda i: (0, 0))]
    return shapes, specs


def _mixer_a(x, p, l, *, chunked, want_v, tm):
    t, d = x.shape
    w_in = p['a_w_in'][l].astype(BF16)
    w_out = p['a_w_out'][l].astype(BF16)
    aw = w_out.shape[0]
    n_e = p['moe_w_router'].shape[-1]
    if chunked:
        ws = p['a_w_s'][l]
        bs = p['a_b_s'][l][:, :, None]
    else:
        gd = aw // A_GROUPS
        ws = jnp.repeat(p['a_w_s'][l][:, 0, 0], gd)[None, :]
        bs = jnp.repeat(p['a_b_s'][l][:, 0], gd)[None, :]
    args = [x, p['g_mix'][l][None, :], w_in, p['a_b_in'][l][None, :], p['a_g_v'][l][None, :],
            p['a_b_v'][l][None, :], ws, bs, w_out, p['g_ffn'][l][None, :],
            p['moe_w_router'][l].astype(BF16), p['moe_b_router'][l][None, :]]
    in_specs = [pl.BlockSpec((tm, d), lambda i: (i, 0))] + [_full(a.shape) for a in args[1:]]
    r_shapes, r_specs = _router_out_specs(t, tm, d, n_e)
    out_shape = [jax.ShapeDtypeStruct((t, d), F32), jax.ShapeDtypeStruct((t, d), F32)]
    out_specs = [pl.BlockSpec((tm, d), lambda i: (i, 0)), pl.BlockSpec((tm, d), lambda i: (i, 0))]
    if want_v:
        out_shape.append(jax.ShapeDtypeStruct((t, aw), F32))
        out_specs.append(pl.BlockSpec((tm, aw), lambda i: (i, 0)))
    out_shape += r_shapes
    out_specs += r_specs
    scratch = [pltpu.VMEM((1, n_e), F32)]
    if chunked:
        scratch.append(pltpu.VMEM((tm, aw), F32))
    outs = pl.pallas_call(
        functools.partial(_mixer_a_kernel, chunked=chunked, want_v=want_v),
        grid=(t // tm,), in_specs=in_specs, out_specs=out_specs, out_shape=out_shape,
        scratch_shapes=scratch, compiler_params=_cparams(1), name="mixer_a_router",
    )(*args)
    outs = list(outs)
    x1, xn = outs[0], outs[1]
    v = outs[2] if want_v else None
    tope, gates, rank, cnt = outs[-4:]
    return x1, xn, v, tope, gates, rank, cnt


def _route_plan(routes, tb):
    n_e = routes[0][2].shape[1]
    n_tok = sum(r[0].shape[0] for r in routes)
    counts = sum(r[2][0] for r in routes)
    pc = (counts + tb - 1) // tb * tb
    pend = jnp.cumsum(pc)
    base = pend - pc
    dests = []
    for tope, rank, cnt in routes:
        dests.append((base[tope] + rank).astype(jnp.int32))
        base = base + cnt[0]
    n_blocks = -(-(n_tok * TOP_K) // tb) + n_e
    blk_start = jnp.arange(n_blocks, dtype=jnp.int32) * tb
    blk_e = jnp.minimum(jnp.sum(pend[None, :] <= blk_start[:, None], axis=1), n_e - 1).astype(jnp.int32)
    n_used = (pend[-1] // tb).astype(jnp.int32)[None]
    return dests, blk_e, n_used, n_blocks


def _dispatch_kernel(dest_ref, xn_ref, xs_in_hbm, xs_hbm, sem, *, td):
    del xs_in_hbm

    def issue(t, carry):
        for k in range(TOP_K):
            d = dest_ref[0, 0, t * TOP_K + k]
            pltpu.make_async_copy(xn_ref.at[pl.ds(t, 1)], xs_hbm.at[pl.ds(d, 1)], sem).start(priority=k % 2)
        return carry
    lax.fori_loop(0, td, issue, 0)
    for _ in range(TOP_K):
        pltpu.make_async_copy(xn_ref, xs_hbm.at[pl.ds(0, td)], sem).wait()


def _dispatch(xn, dest, rows, td):
    t, d = xn.shape
    dest3 = dest.reshape(t // td, 1, td * TOP_K)
    return pl.pallas_call(
        functools.partial(_dispatch_kernel, td=td),
        grid=(t // td,),
        in_specs=[pl.BlockSpec((1, 1, td * TOP_K), lambda i: (i, 0, 0), memory_space=pltpu.SMEM),
                  pl.BlockSpec((td, d), lambda i: (i, 0)), pl.BlockSpec(memory_space=pl.ANY)],
        out_specs=pl.BlockSpec(memory_space=pl.ANY),
        out_shape=jax.ShapeDtypeStruct(rows.shape, F32),
        scratch_shapes=[pltpu.SemaphoreType.DMA(())],
        input_output_aliases={2: 0},
        compiler_params=_cparams(1), name="moe_dispatch",
    )(dest3, xn, rows)


def _experts_kernel(be_ref, nu_ref, xs_ref, wgu_ref, bgu_ref, wdn_ref, bdn_ref, y_ref, wgu_bf, wdn_bf):
    i = pl.program_id(0)
    live = i < nu_ref[0]
    f = wdn_ref.shape[2]
    changed = jnp.logical_or(i == 0, be_ref[i] != be_ref[jnp.maximum(i - 1, 0)])

    @pl.when(jnp.logical_and(live, changed))
    def _():
        wgu_bf[...] = wgu_ref[0, 0].astype(BF16)
        wdn_bf[...] = wdn_ref[0, 0].astype(BF16)

    @pl.when(live)
    def _():
        x = xs_ref[...]
        hgu = _mm(x, wgu_bf[...]) + bgu_ref[0, 0]
        gate = jnp.minimum(hgu[:, :f], SWIGLU_LIMIT)
        up = jnp.clip(hgu[:, f:], -SWIGLU_LIMIT, SWIGLU_LIMIT)
        act = gate * jax.nn.sigmoid(SWIGLU_ALPHA * gate) * (up + 1.0)
        y_ref[...] = _mm(act, wdn_bf[...]) + bdn_ref[0, 0]

    @pl.when(jnp.logical_not(live))
    def _():
        y_ref[...] = jnp.zeros_like(y_ref)


def _experts(xs, blk_e, n_used, n_blocks, p, l, tb):
    n_rows, d = xs.shape
    wgu, wdn = p['moe_w_gu'], p['moe_w_down']
    _, n_e, _, f2 = wgu.shape
    f = f2 // 2
    bgu = p['moe_b_gu'][:, :, None, :]
    bdn = p['moe_b_down'][:, :, None, :]

    def row_map(i, be, nu):
        return (jnp.minimum(i, nu[0] - 1), 0)

    def w_map(i, be, nu):
        return (l, be[jnp.minimum(i, nu[0] - 1)], 0, 0)

    return pl.pallas_call(
        _experts_kernel,
        grid_spec=pltpu.PrefetchScalarGridSpec(
            num_scalar_prefetch=2, grid=(n_blocks,),
            in_specs=[pl.BlockSpec((tb, d), row_map), pl.BlockSpec((1, 1, d, f2), w_map),
                      pl.BlockSpec((1, 1, 1, f2), w_map), pl.BlockSpec((1, 1, f, d), w_map),
                      pl.BlockSpec((1, 1, 1, d), w_map)],
            out_specs=pl.BlockSpec((tb, d), lambda i, be, nu: (i, 0)),
            scratch_shapes=[pltpu.VMEM((d, f2), BF16), pltpu.VMEM((f, d), BF16)]),
        out_shape=jax.ShapeDtypeStruct((n_rows, d), F32),
        compiler_params=_cparams(1), name="moe_experts",
    )(blk_e, n_used, xs, wgu, bgu, wdn, bdn)


def _moe_rows(groups, p, l, spare_rows=None):
    tb = PROMPT_EXPERT_BLOCK
    dests, blk_e, n_used, n_blocks = _route_plan([g[1:4] for g in groups], tb)
    xs = jnp.zeros((n_blocks * tb, groups[0][0].shape[1]), F32) if spare_rows is None else spare_rows
    for (xn, _, _, _, td), dest in zip(groups, dests):
        xs = _dispatch(xn, dest, xs, td)
    return _experts(xs, blk_e, n_used, n_blocks, p, l, tb), dests, xs


def _combine_kvq_kernel(dest_ref, x1_ref, gate_ref, y_hbm, gkv_ref, wk_ref, wv_ref, gmix_ref, wq_ref,
                        x2_ref, k_ref, v_ref, *rest, for_attn):
    if for_attn:
        qt_ref, kb_ref, vt_ref, ybuf, sem = rest
    else:
        q_ref, ybuf, sem = rest
    tm = x1_ref.shape[0]
    x2 = x1_ref[...] + _gather_combine(dest_ref, y_hbm, ybuf, sem, gate_ref[...], tm)
    x2_ref[...] = x2
    hk = _rms(x2, gkv_ref[...])
    k = _mm(hk, wk_ref[...])
    v = _mm(hk, wv_ref[...])
    k_ref[...] = k
    v_ref[...] = v
    q = _mm(_rms(x2, gmix_ref[...]), wq_ref[...])
    if for_attn:
        n_h = vt_ref.shape[0]
        qt_ref[:, 0] = (q * (ATTN_SCALE * LOG2E)).T.reshape(n_h, V7X_LANES, tm).astype(BF16)
        kb_ref[...] = k.astype(BF16)
        vt_ref[:, 0, :V_DIM] = v.T.reshape(n_h, V_DIM, tm).astype(BF16)
        ones_row = lax.broadcasted_iota(jnp.int32, (n_h, VT_PAD_ROWS, tm), 1) == 0
        vt_ref[:, 0, V_DIM:] = jnp.where(ones_row, 1.0, 0.0).astype(BF16)
    else:
        q_ref[...] = q


def _combine_kvq(x1, gates, y_rows, dest, p, *, for_attn, tm):
    t, d = x1.shape
    dest3 = dest.reshape(t // tm, 1, tm * TOP_K)
    args = [dest3, x1, gates, y_rows, p['g_kv'][None, :], p['w_k'].astype(BF16), p['w_v'].astype(BF16),
            p['g_mix'][N_A_LAYERS][None, :], p['b_w_q'][0].astype(BF16)]
    tile = pl.BlockSpec((tm, d), lambda i: (i, 0))
    in_specs = [pl.BlockSpec((1, 1, tm * TOP_K), lambda i: (i, 0, 0), memory_space=pltpu.SMEM),
                tile, pl.BlockSpec((tm, TOP_K), lambda i: (i, 0)), pl.BlockSpec(memory_space=pl.ANY)]
    in_specs += [_full(a.shape) for a in args[4:]]
    out_shape = [jax.ShapeDtypeStruct((t, d), F32)] * 3
    out_specs = [tile] * 3
    if for_attn:
        n_h = d // V7X_LANES
        out_shape += [jax.ShapeDtypeStruct((n_h, t // tm, V7X_LANES, tm), BF16), jax.ShapeDtypeStruct((t, d), BF16),
                      jax.ShapeDtypeStruct((n_h, t // tm, V_DIM + VT_PAD_ROWS, tm), BF16)]
        out_specs += [pl.BlockSpec((n_h, 1, V7X_LANES, tm), lambda i: (0, i, 0, 0)), tile,
                      pl.BlockSpec((n_h, 1, V_DIM + VT_PAD_ROWS, tm), lambda i: (0, i, 0, 0))]
    else:
        out_shape += [jax.ShapeDtypeStruct((t, d), F32)]
        out_specs += [tile]
    return pl.pallas_call(
        functools.partial(_combine_kvq_kernel, for_attn=for_attn),
        grid=(t // tm,), in_specs=in_specs, out_specs=out_specs, out_shape=out_shape,
        scratch_shapes=[pltpu.VMEM((TOP_K, tm, d), F32), pltpu.SemaphoreType.DMA(())],
        compiler_params=_cparams(1), name="moe_combine_kvq",
    )(*args)


def _bucket_np(n):
    n = np.asarray(n)
    nf = np.maximum(n, 1).astype(np.float32)
    large = MAX_EXACT + (np.log(nf / np.float32(MAX_EXACT)) / np.float32(math.log(MAX_DISTANCE / MAX_EXACT))
                         * np.float32(N_BUCKETS - MAX_EXACT)).astype(np.int32)
    return np.where(n < MAX_EXACT, n, np.minimum(large, N_BUCKETS - 1)).astype(np.int32)


def _bias_tiles_kernel(tbl_ref, bkt_ref, bkt_s_ref, tile_ref, last_ref, zero_ref):
    h = pl.program_id(0)
    far = tbl_ref[N_BUCKETS - 1, h]
    for dl in range(2):
        b = bkt_ref[dl]
        acc = jnp.where(b < 0, NEG_BIG, 0.0).astype(F32)
        for bb in range(N_BUCKETS - 1):
            acc = jnp.where(b == bb, (tbl_ref[bb, h] - far) * LOG2E, acc)
        tile_ref[0, dl] = acc

    @pl.when(h == 0)
    def _():
        last_ref[...] = jnp.zeros_like(last_ref)
        zero_ref[...] = jnp.zeros_like(zero_ref)

    bs = bkt_s_ref[...]
    val = jnp.zeros(bs.shape, F32)
    for bb in range(N_BUCKETS - 1):
        val = jnp.where(bs == bb, tbl_ref[bb, h] - far, val)
    row = lax.broadcasted_iota(jnp.int32, last_ref.shape, 0) % N_HEADS
    last_ref[...] = jnp.where(row == h, val, last_ref[...])
    zero_ref[...] = jnp.where(row == h, tbl_ref[0, h] - far, zero_ref[...])


def _bias_tiles(rel_table, tb, page):
    assert tb >= MAX_DISTANCE and page >= MAX_DISTANCE and page == V7X_LANES
    kq = np.arange(tb)
    tiles = []
    for dl in range(2):
        dist = dl * tb + kq[None, :] - kq[:, None]
        tiles.append(np.where(dist >= 0, _bucket_np(np.maximum(dist, 0)), -1))
    bkt = jnp.asarray(np.stack(tiles).astype(np.int32))
    bkt_s = jnp.asarray(_bucket_np(page - np.arange(page))[None, :].astype(np.int32))
    rows = (2 * N_HEADS, V7X_LANES)
    return pl.pallas_call(
        _bias_tiles_kernel,
        grid=(N_HEADS,),
        in_specs=[pl.BlockSpec(memory_space=pltpu.SMEM), _full(bkt.shape), _full(bkt_s.shape)],
        out_specs=[pl.BlockSpec((1, 2, tb, tb), lambda h: (h, 0, 0, 0)),
                   pl.BlockSpec(rows, lambda h: (0, 0)), pl.BlockSpec(rows, lambda h: (0, 0))],
        out_shape=[jax.ShapeDtypeStruct((N_HEADS, 2, tb, tb), F32),
                   jax.ShapeDtypeStruct(rows, F32), jax.ShapeDtypeStruct(rows, F32)],
        compiler_params=_cparams(1), name="rel_bias_tiles",
    )(rel_table, bkt, bkt_s)


def _attn_prompt_kernel(lam_ref, q_ref, k_ref, vt_ref, bias_ref, o_ref, acc_ref, m_ref, sa_ref, sb_ref,
                        pa_ref, pb_ref, *, tb):
    qi = pl.program_id(1)
    qt = q_ref[0, 0]
    row = lax.broadcasted_iota(jnp.int32, qt.shape, 0)
    zero = jnp.zeros_like(qt)
    qcat = jnp.concatenate([jnp.where(row < HEAD_DIM, qt, zero), jnp.where(row >= HEAD_DIM, qt, zero)], axis=1)
    acc_ref[...] = jnp.zeros_like(acc_ref)

    def scores(j, bias=None):
        k_blk = k_ref[pl.ds(pl.multiple_of(j * tb, tb), tb), :]
        s = jnp.dot(k_blk, qcat, preferred_element_type=F32)
        if bias is not None:
            s = s + jnp.concatenate([bias, bias], axis=1)
        return s, jnp.max(s, axis=0, keepdims=True)

    def absorb(s_ref, p_ref, s_max, j, m_old):
        m_new = jnp.maximum(m_old, s_max)
        for c in range(0, tb, ATTN_ROW_CHUNK):
            p_ref[c:c + ATTN_ROW_CHUNK, :] = jnp.exp2((s_ref[c:c + ATTN_ROW_CHUNK, :] - m_new).astype(BF16))
        acc_ref[...] = (jnp.exp2(m_old - m_new) * acc_ref[...]
                        + jnp.dot(vt_ref[0, j], p_ref[...], preferred_element_type=F32))
        return m_new

    n_far = jnp.maximum(qi - 1, 0)
    n_pairs = lax.shift_right_logical(n_far, 1)
    neg = jnp.full((1, 2 * tb), NEG_BIG, F32)
    m_ref[...] = neg

    @pl.when(n_pairs > 0)
    def _():
        sa_ref[...], m_ref[...] = scores(0)

    def pair(i, carry):
        m_old, max_a = carry
        a = 2 * i
        sb_ref[...], max_b = scores(a + 1)
        m_old = absorb(sa_ref, pa_ref, max_a, a, m_old)
        sa_ref[...], max_a = scores(a + 2)
        m_old = absorb(sb_ref, pb_ref, max_b, a + 1, m_old)
        return m_old, max_a

    m_ref[...], _ = lax.fori_loop(0, n_pairs, pair, (neg, m_ref[...]))

    def single(j, bias=None):
        sa_ref[...], s_max = scores(j, bias)
        m_ref[...] = absorb(sa_ref, pa_ref, s_max, j, m_ref[...])

    @pl.when(n_far > 2 * n_pairs)
    def _():
        single(2 * n_pairs)

    @pl.when(qi >= 1)
    def _():
        single(qi - 1, bias_ref[0, 1])
    single(qi, bias_ref[0, 0])
    o_all = acc_ref[:V_DIM, :] / acc_ref[V_DIM:V_DIM + 1, :]
    o_ref[...] = (o_all[:, :tb] - lam_ref[0] * o_all[:, tb:]).T


def _attn_prompt(qt4, kb, vt4, bias_tiles, lam, tb):
    t, d = kb.shape
    n_h = d // V7X_LANES
    n_q = t // tb
    vt_rows = vt4.shape[2]
    return pl.pallas_call(
        functools.partial(_attn_prompt_kernel, tb=tb),
        grid=(n_h, n_q),
        in_specs=[pl.BlockSpec(memory_space=pltpu.SMEM),
                  pl.BlockSpec((1, 1, V7X_LANES, tb), lambda h, i: (h, i, 0, 0)),
                  pl.BlockSpec((t, V7X_LANES), lambda h, i: (0, h)),
                  pl.BlockSpec((1, n_q, vt_rows, tb), lambda h, i: (h, 0, 0, 0)),
                  pl.BlockSpec((1, 2, tb, tb), lambda h, i: (h, 0, 0, 0))],
        out_specs=pl.BlockSpec((tb, V7X_LANES), lambda h, i: (i, h)),
        out_shape=jax.ShapeDtypeStruct((t, d), F32),
        scratch_shapes=[pltpu.VMEM((vt_rows, 2 * tb), F32), pltpu.VMEM((1, 2 * tb), F32),
                        pltpu.VMEM((tb, 2 * tb), F32), pltpu.VMEM((tb, 2 * tb), F32),
                        pltpu.VMEM((tb, 2 * tb), BF16), pltpu.VMEM((tb, 2 * tb), BF16)],
        compiler_params=_cparams(2), name="attn_prompt",
    )(lam, qt4, kb, vt4, bias_tiles)


def _attn_sample_kernel(pt_ref, lam_ref, qbd_ref, q_ref, kn_ref, vn_ref, hmask_ref, expand_ref, blast_ref,
                        bzero_ref, *rest, n_pp):
    k_refs = rest[:n_pp]
    v_refs = rest[n_pp:2 * n_pp]
    o_ref, acc_ref, m_ref, l_ref = rest[2 * n_pp:]
    del pt_ref
    g = pl.program_id(1)
    n_g = pl.num_programs(1)
    n_h = N_HEADS

    @pl.when(g == 0)
    def _():
        qv = q_ref[0].astype(BF16).astype(F32)
        s0 = jnp.sum(qv * kn_ref[0].astype(BF16).astype(F32), axis=-1, keepdims=True)
        m_ref[...] = s0 + bzero_ref[:, 0:1]
        l_ref[...] = jnp.ones_like(l_ref)
        vn = vn_ref[0].astype(BF16).astype(F32)
        acc_ref[...] = jnp.concatenate([vn, vn], axis=0)

    qbd = qbd_ref[0].astype(BF16)
    parts = []
    for i in range(n_pp):
        kst = k_refs[i][0]
        kst = kst.reshape(kst.shape[0] * kst.shape[1], kst.shape[2]).astype(BF16)
        parts.append(jnp.dot(qbd, kst, preferred_element_type=F32))
    parts[-1] = parts[-1] + jnp.where(g == n_g - 1, 1.0, 0.0) * blast_ref[...]
    s = jnp.concatenate(parts, axis=1)
    m_old = m_ref[...]
    m_new = jnp.maximum(m_old, jnp.max(s, axis=-1, keepdims=True))
    alpha = jnp.exp(m_old - m_new)
    pr = jnp.exp(s - m_new)
    l_ref[...] = alpha * l_ref[...] + jnp.sum(pr, axis=-1, keepdims=True)
    m_ref[...] = m_new
    keys = pr.shape[1] // n_pp
    n_m = pr.shape[0]
    p_rows = jnp.concatenate([pr[:, i * keys:(i + 1) * keys] for i in range(n_pp)], axis=0).astype(BF16)
    spread = jnp.dot(p_rows, expand_ref[...], preferred_element_type=F32) * hmask_ref[...]
    spread = spread.astype(BF16)
    acc = alpha * acc_ref[...]
    for i in range(n_pp):
        acc = acc + jnp.dot(spread[i * n_m:(i + 1) * n_m], v_refs[i][0].astype(BF16),
                            preferred_element_type=F32)
    acc_ref[...] = acc

    @pl.when(g == n_g - 1)
    def _():
        o = acc_ref[...] / l_ref[...]
        o_ref[0] = o[:n_h] - lam_ref[0] * o[n_h:]


def _attn_sample(q, k_new, v_new, cache_k, cache_v, page_table, blast, bzero, lam):
    nb, d = q.shape
    n_phys, page = cache_k.shape[0], cache_k.shape[1]
    n_pages = page_table.shape[1]
    n_pp = math.gcd(PAGES_PER_STEP, n_pages)
    n_h = N_HEADS
    n_m = 2 * n_h
    perm = np.concatenate([np.arange(0, n_m, 2), np.arange(1, n_m, 2)])
    qp = (q * ATTN_SCALE).reshape(nb, n_m, HEAD_DIM)[:, perm]
    kn = k_new.reshape(nb, n_m, HEAD_DIM)[:, perm]
    vn = v_new.reshape(nb, n_h, V_DIM)
    own = jnp.asarray((perm[:, None] == np.arange(n_m)[None, :])[None, :, :, None])
    qbd = jnp.where(own, qp[:, :, None, :], 0.0).reshape(nb, n_m, n_m * HEAD_DIM)
    ckt = jnp.transpose(cache_k, (0, 2, 3, 1))
    cv = cache_v.reshape(n_phys, page * n_h, V_DIM)
    lanes = np.arange(page * n_h)
    hmask = jnp.asarray((lanes[None, :] % n_h == np.arange(n_pp * n_m)[:, None] % n_h).astype(np.float32))
    expand = jnp.asarray((lanes[None, :] // n_h == np.arange(page)[:, None]).astype(np.float32), dtype=BF16)

    def page_map(i):
        return lambda b, g, pt: (pt[b * n_pages + g * n_pp + i], 0, 0, 0)

    def page_map3(i):
        return lambda b, g, pt: (pt[b * n_pages + g * n_pp + i], 0, 0)
    row3 = lambda b, g, pt: (b, 0, 0)
    const2 = lambda b, g, pt: (0, 0)
    in_specs = [pl.BlockSpec(memory_space=pltpu.SMEM),
                pl.BlockSpec((1, n_m, n_m * HEAD_DIM), row3),
                pl.BlockSpec((1, n_m, HEAD_DIM), row3), pl.BlockSpec((1, n_m, HEAD_DIM), row3),
                pl.BlockSpec((1, n_h, V_DIM), row3), pl.BlockSpec(hmask.shape, const2),
                pl.BlockSpec(expand.shape, const2), pl.BlockSpec(blast.shape, const2),
                pl.BlockSpec(bzero.shape, const2)]
    in_specs += [pl.BlockSpec((1, n_m, HEAD_DIM, page), page_map(i)) for i in range(n_pp)]
    in_specs += [pl.BlockSpec((1, page * n_h, V_DIM), page_map3(i)) for i in range(n_pp)]
    out = pl.pallas_call(
        functools.partial(_attn_sample_kernel, n_pp=n_pp),
        grid_spec=pltpu.PrefetchScalarGridSpec(
            num_scalar_prefetch=1, grid=(nb, n_pages // n_pp), in_specs=in_specs,
            out_specs=pl.BlockSpec((1, n_h, V_DIM), row3),
            scratch_shapes=[pltpu.VMEM((n_m, V_DIM), F32), pltpu.VMEM((n_m, 1), F32),
                            pltpu.VMEM((n_m, 1), F32)]),
        out_shape=jax.ShapeDtypeStruct((nb, n_h, V_DIM), F32),
        compiler_params=_cparams(2), name="attn_sample",
    )(page_table.reshape(-1).astype(jnp.int32), lam, qbd, qp, kn, vn, hmask, expand, blast, bzero,
      *([ckt] * n_pp), *([cv] * n_pp))
    return out.reshape(nb, d)


def _attn_out_kernel(o_ref, x2_ref, gsub_ref, wo_ref, gffn_ref, wr_ref, br_ref,
                     x3_ref, xn_ref, tope_ref, gate_ref, rank_ref, cnt_ref, runcnt_ref, on_ref,
                     *, sub_scale):
    @pl.when(pl.program_id(0) == 0)
    def _():
        runcnt_ref[...] = jnp.zeros_like(runcnt_ref)

    vd = gsub_ref.shape[1]
    for h in range(o_ref.shape[1] // vd):
        oh = o_ref[:, h * vd:(h + 1) * vd]
        on_ref[:, h * vd:(h + 1) * vd] = _rms(oh, gsub_ref[...]) * sub_scale
    x3 = x2_ref[...] + _mm(on_ref[...], wo_ref[...])
    x3_ref[...] = x3
    _router_tail(x3, gffn_ref[...], wr_ref[...], br_ref[...], runcnt_ref,
                 xn_ref, tope_ref, gate_ref, rank_ref, cnt_ref)


def _attn_out(o, x2, p, l, sub_scale, *, tm):
    t, d = x2.shape
    n_e = p['moe_w_router'].shape[-1]
    args = [o, x2, p['b_g_sub'][0][None, :], p['b_w_o'][0].astype(BF16), p['g_ffn'][l][None, :],
            p['moe_w_router'][l].astype(BF16), p['moe_b_router'][l][None, :]]
    tile = pl.BlockSpec((tm, d), lambda i: (i, 0))
    r_shapes, r_specs = _router_out_specs(t, tm, d, n_e)
    outs = pl.pallas_call(
        functools.partial(_attn_out_kernel, sub_scale=sub_scale),
        grid=(t // tm,), in_specs=[tile, tile] + [_full(a.shape) for a in args[2:]],
        out_specs=[tile, tile] + r_specs,
        out_shape=[jax.ShapeDtypeStruct((t, d), F32)] * 2 + r_shapes,
        scratch_shapes=[pltpu.VMEM((1, n_e), F32), pltpu.VMEM((tm, d), F32)],
        compiler_params=_cparams(1), name="attn_out_router",
    )(*args)
    return outs


def _combine_final_kernel(dest_ref, x3_ref, gate_ref, y_hbm, gout_ref, y_ref, ybuf, sem):
    tm = x3_ref.shape[0]
    x4 = x3_ref[...] + _gather_combine(dest_ref, y_hbm, ybuf, sem, gate_ref[...], tm)
    y_ref[...] = _rms(x4, gout_ref[...])


def _combine_final(x3, gates, y_rows, dest, g_out, tm):
    t, d = x3.shape
    dest3 = dest.reshape(t // tm, 1, tm * TOP_K)
    tile = pl.BlockSpec((tm, d), lambda i: (i, 0))
    return pl.pallas_call(
        _combine_final_kernel,
        grid=(t // tm,),
        in_specs=[pl.BlockSpec((1, 1, tm * TOP_K), lambda i: (i, 0, 0), memory_space=pltpu.SMEM),
                  tile, pl.BlockSpec((tm, TOP_K), lambda i: (i, 0)), pl.BlockSpec(memory_space=pl.ANY),
                  _full((1, d))],
        out_specs=tile, out_shape=jax.ShapeDtypeStruct((t, d), F32),
        scratch_shapes=[pltpu.VMEM((TOP_K, tm, d), F32), pltpu.SemaphoreType.DMA(())],
        compiler_params=_cparams(1), name="moe_combine_final",
    )(dest3, x3, gates, y_rows, g_out[None, :])


def kernel(x_prompt, x_sample, cache_k, cache_v, page_table, g_mix, g_ffn, g_kv, g_out, a_w_in, a_b_in, a_g_v, a_b_v, a_w_s, a_b_s, a_w_out, w_k, w_v, b_w_q, b_lq1, b_lk1, b_lq2, b_lk2, b_g_sub, b_w_o, rel_table, moe_w_router, moe_b_router, moe_w_gu, moe_b_gu, moe_w_down, moe_b_down):
    p = {
        'g_mix': g_mix, 'g_ffn': g_ffn, 'g_kv': g_kv, 'g_out': g_out,
        'a_w_in': a_w_in, 'a_b_in': a_b_in, 'a_g_v': a_g_v, 'a_b_v': a_b_v,
        'a_w_s': a_w_s, 'a_b_s': a_b_s, 'a_w_out': a_w_out,
        'w_k': w_k, 'w_v': w_v, 'b_w_q': b_w_q, 'b_g_sub': b_g_sub, 'b_w_o': b_w_o,
        'moe_w_router': moe_w_router, 'moe_b_router': moe_b_router,
        'moe_w_gu': moe_w_gu, 'moe_b_gu': moe_b_gu, 'moe_w_down': moe_w_down, 'moe_b_down': moe_b_down,
    }
    assert x_prompt.shape[0] == 1 and x_sample.shape[1] == 1
    b, s, d = x_prompt.shape
    nb = x_sample.shape[0]
    page = cache_k.shape[1]
    lam_init = 0.8 - 0.6 * math.exp(-0.3 * N_A_LAYERS)
    lam = (jnp.exp(jnp.sum(b_lq1[0] * b_lk1[0])) - jnp.exp(jnp.sum(b_lq2[0] * b_lk2[0])) + lam_init)
    lam = lam.astype(F32)[None]
    tb = min(ATTN_BLOCK, s)
    bias_tiles, blast, bzero = _bias_tiles(rel_table, tb, page)
    tm_p, tm_s = PROMPT_TOKEN_TILE, nb
    assert s % tm_p == 0 and tm_p % CHUNK == 0 and s % tb == 0

    x1_p, xn_p, _, tope_p, gates_p, rank_p, cnt_p = _mixer_a(
        x_prompt[0], p, 0, chunked=True, want_v=False, tm=tm_p)
    x1_s, xn_s, cv_s, tope_s, gates_s, rank_s, cnt_s = _mixer_a(
        x_sample[:, 0], p, 0, chunked=False, want_v=True, tm=tm_s)
    y_rows, (dest_p, dest_s), rows_buf = _moe_rows(
        [(xn_p, tope_p, rank_p, cnt_p, tm_p), (xn_s, tope_s, rank_s, cnt_s, tm_s)], p, 0)
    x2_p, k_p, v_p, qt4, kb, vt4 = _combine_kvq(x1_p, gates_p, y_rows, dest_p, p, for_attn=True, tm=tb)
    x2_s, k_s, v_s, q_s = _combine_kvq(x1_s, gates_s, y_rows, dest_s, p, for_attn=False, tm=tm_s)

    o_p = _attn_prompt(qt4, kb, vt4, bias_tiles, lam, tb)
    o_s = _attn_sample(q_s, k_s, v_s, cache_k, cache_v, page_table, blast, bzero, lam)
    x3_p, xn_p, tope_p, gates_p, rank_p, cnt_p = _attn_out(o_p, x2_p, p, 1, 1.0 - lam_init, tm=tm_p)
    x3_s, xn_s, tope_s, gates_s, rank_s, cnt_s = _attn_out(o_s, x2_s, p, 1, 1.0 - lam_init, tm=tm_s)
    y_rows, (dest_p, dest_s), _ = _moe_rows(
        [(xn_p, tope_p, rank_p, cnt_p, tm_p), (xn_s, tope_s, rank_s, cnt_s, tm_s)], p, 1, rows_buf)
    y_p = _combine_final(x3_p, gates_p, y_rows, dest_p, g_out, tm_p)
    y_s = _combine_final(x3_s, gates_s, y_rows, dest_s, g_out, tm_s)
    n_h = N_HEADS
    return (y_p[None], y_s[:, None],
            k_p.reshape(1, s, 2 * n_h, HEAD_DIM), v_p.reshape(1, s, n_h, V_DIM),
            k_s.reshape(nb, 1, 2 * n_h, HEAD_DIM), v_s.reshape(nb, 1, n_h, V_DIM),
            cv_s.reshape(1, nb, 1, -1))
```

```python
import functools
import math

import numpy as np
import jax
import jax.numpy as jnp
from jax import lax
from jax.experimental import pallas as pl
from jax.experimental.pallas import tpu as pltpu

F32 = jnp.float32
BF16 = jnp.bfloat16

EPS = 1e-6
CHUNK = 128
A_GROUPS = 8
N_HEADS = 8
HEAD_DIM = 64
V_DIM = 2 * HEAD_DIM
ATTN_SCALE = HEAD_DIM ** -0.5
N_BUCKETS = 32
MAX_EXACT = N_BUCKETS // 2
MAX_DISTANCE = 128
TOP_K = 4
SWIGLU_LIMIT = 7.0
SWIGLU_ALPHA = 1.702
N_A_LAYERS = 1

V7X_LANES = 128
V7X_SUBLANES = 8
V7X_VMEM_LIMIT_BYTES = 56 * 1024 * 1024

NEG_BIG = -1e30
PROMPT_TOKEN_TILE = 512
PROMPT_EXPERT_BLOCK = 512
ATTN_BLOCK = 512
ATTN_ROW_CHUNK = 64
VT_PAD_ROWS = 16
LOG2E = math.log2(math.e)
PAGES_PER_STEP = 8


def _cparams(n_axes):
    return pltpu.CompilerParams(
        dimension_semantics=("arbitrary",) * n_axes,
        vmem_limit_bytes=V7X_VMEM_LIMIT_BYTES,
    )


def _mm(a, w):
    return jnp.dot(a.astype(BF16), w.astype(BF16), preferred_element_type=F32)


def _rms(x, g):
    return x * lax.rsqrt(jnp.mean(x * x, axis=-1, keepdims=True) + EPS) * g


def _gelu(x):
    return 0.5 * x * (1.0 + lax.erf(x * (2.0 ** -0.5)))


def _cols4(c0, c1, c2, c3):
    m = c0.shape[0]
    lane = lax.broadcasted_iota(jnp.int32, (m, TOP_K), 1)
    return jnp.where(lane == 0, c0, jnp.where(lane == 1, c1, jnp.where(lane == 2, c2, c3)))


def _router_tail(x1, gffn, wr, br, runcnt_ref, xn_ref, tope_ref, gate_ref, rank_ref, cnt_ref):
    m = x1.shape[0]
    xn = _rms(x1, gffn)
    xn_ref[...] = xn
    logits = _mm(xn, wr) + br
    n_e = logits.shape[1]
    lane = lax.broadcasted_iota(jnp.int32, (m, n_e), 1).astype(F32)
    vals, sels, ohs = [], [], []
    cur = logits
    for _ in range(TOP_K):
        mx = jnp.max(cur, axis=-1, keepdims=True)
        sel = jnp.min(jnp.where(cur == mx, lane, float(n_e)), axis=-1, keepdims=True)
        oh = lane == sel
        vals.append(mx)
        sels.append(sel)
        ohs.append(oh.astype(F32))
        cur = jnp.where(oh, -jnp.inf, cur)
    es = [jnp.exp(v - vals[0]) for v in vals]
    den = es[0] + es[1] + es[2] + es[3]
    gate_ref[...] = _cols4(*[e / den for e in es])
    tope_ref[...] = _cols4(*sels).astype(jnp.int32)
    row = lax.broadcasted_iota(jnp.int32, (m, m), 0)
    col = lax.broadcasted_iota(jnp.int32, (m, m), 1)
    ltri = (row > col).astype(BF16)
    prior = runcnt_ref[...]
    ranks = []
    for oh in ohs:
        within = jnp.dot(ltri, oh.astype(BF16), preferred_element_type=F32)
        ranks.append(jnp.sum((prior + within) * oh, axis=-1, keepdims=True))
        prior = prior + jnp.sum(oh, axis=0, keepdims=True)
    rank_ref[...] = _cols4(*ranks).astype(jnp.int32)
    runcnt_ref[...] = prior
    cnt_ref[...] = prior.astype(jnp.int32)


def _gather_combine(dest_ref, y_hbm, ybuf, sem, gates, tm):
    def issue(t, carry):
        for k in range(TOP_K):
            d = dest_ref[0, 0, t * TOP_K + k]
            pltpu.make_async_copy(y_hbm.at[pl.ds(d, 1)], ybuf.at[k, pl.ds(t, 1)], sem).start(priority=k % 2)
        return carry
    lax.fori_loop(0, tm, issue, 0)
    for k in range(TOP_K):
        pltpu.make_async_copy(y_hbm.at[pl.ds(0, tm)], ybuf.at[k], sem).wait()
    acc = gates[:, 0:1] * ybuf[0]
    for k in range(1, TOP_K):
        acc = acc + gates[:, k:k + 1] * ybuf[k]
    return acc


def _mixer_a_kernel(*refs, chunked, want_v):
    (x_ref, gmix_ref, win_ref, bin_ref, gv_ref, bv_ref, ws_ref, bs_ref, wout_ref,
     gffn_ref, wr_ref, br_ref) = refs[:12]
    rest = list(refs[12:])
    x1_ref, xn_ref = rest[0], rest[1]
    rest = rest[2:]
    v_ref = rest.pop(0) if want_v else None
    tope_ref, gate_ref, rank_ref, cnt_ref, runcnt_ref = rest[:5]
    us_ref = rest[5] if chunked else None

    @pl.when(pl.program_id(0) == 0)
    def _():
        runcnt_ref[...] = jnp.zeros_like(runcnt_ref)

    x = x_ref[...]
    tm = x.shape[0]
    aw = wout_ref.shape[0]
    h = _rms(x, gmix_ref[...])
    u = _gelu(_mm(h, win_ref[:, :aw]) + bin_ref[:, :aw])
    vr = _gelu(_mm(h, win_ref[:, aw:]) + bin_ref[:, aw:])
    mu = jnp.mean(vr, axis=-1, keepdims=True)
    vc = vr - mu
    v = vc * lax.rsqrt(jnp.mean(vc * vc, axis=-1, keepdims=True) + EPS) * gv_ref[...] + bv_ref[...]
    if want_v:
        v_ref[...] = v
    if chunked:
        n_g, cl = ws_ref.shape[0], ws_ref.shape[1]
        gd = aw // n_g
        tri = (lax.broadcasted_iota(jnp.int32, (cl, cl), 0) >= lax.broadcasted_iota(jnp.int32, (cl, cl), 1))
        for g in range(n_g):
            wm = jnp.where(tri, ws_ref[g], 0.0).astype(BF16)
            for c in range(tm // cl):
                vg = v[c * cl:(c + 1) * cl, g * gd:(g + 1) * gd].astype(BF16)
                s = jnp.dot(wm, vg, preferred_element_type=F32) + bs_ref[g]
                us_ref[c * cl:(c + 1) * cl, g * gd:(g + 1) * gd] = u[c * cl:(c + 1) * cl, g * gd:(g + 1) * gd] * s
        us = us_ref[...]
    else:
        us = u * (v * ws_ref[...] + bs_ref[...])
    x1 = x + _mm(us, wout_ref[...])
    x1_ref[...] = x1
    _router_tail(x1, gffn_ref[...], wr_ref[...], br_ref[...], runcnt_ref,
                 xn_ref, tope_ref, gate_ref, rank_ref, cnt_ref)


def _full(shape):
    nd = len(shape)
    return pl.BlockSpec(shape, lambda *_: (0,) * nd, pipeline_mode=pl.Buffered(1))


def _router_out_specs(t, tm, d, n_e):
    shapes = [jax.ShapeDtypeStruct((t, TOP_K), jnp.int32), jax.ShapeDtypeStruct((t, TOP_K), F32),
              jax.ShapeDtypeStruct((t, TOP_K), jnp.int32), jax.ShapeDtypeStruct((1, n_e), jnp.int32)]
    specs = [pl.BlockSpec((tm, TOP_K), lambda i: (i, 0)), pl.BlockSpec((tm, TOP_K), lambda i: (i, 0)),
             pl.BlockSpec((tm, TOP_K), lambda i: (i, 0)), pl.BlockSpec((1, n_e), lambda i: (0, 0))]
    return shapes, specs


def _mixer_a(x, p, l, *, chunked, want_v, tm):
    t, d = x.shape
    w_in = p['a_w_in'][l].astype(BF16)
    w_out = p['a_w_out'][l].astype(BF16)
    aw = w_out.shape[0]
    n_e = p['moe_w_router'].shape[-1]
    if chunked:
        ws = p['a_w_s'][l]
        bs = p['a_b_s'][l][:, :, None]
    else:
        gd = aw // A_GROUPS
        ws = jnp.repeat(p['a_w_s'][l][:, 0, 0], gd)[None, :]
        bs = jnp.repeat(p['a_b_s'][l][:, 0], gd)[None, :]
    args = [x, p['g_mix'][l][None, :], w_in, p['a_b_in'][l][None, :], p['a_g_v'][l][None, :],
            p['a_b_v'][l][None, :], ws, bs, w_out, p['g_ffn'][l][None, :],
            p['moe_w_router'][l].astype(BF16), p['moe_b_router'][l][None, :]]
    in_specs = [pl.BlockSpec((tm, d), lambda i: (i, 0))] + [_full(a.shape) for a in args[1:]]
    r_shapes, r_specs = _router_out_specs(t, tm, d, n_e)
    out_shape = [jax.ShapeDtypeStruct((t, d), F32), jax.ShapeDtypeStruct((t, d), F32)]
    out_specs = [pl.BlockSpec((tm, d), lambda i: (i, 0)), pl.BlockSpec((tm, d), lambda i: (i, 0))]
    if want_v:
        out_shape.append(jax.ShapeDtypeStruct((t, aw), F32))
        out_specs.append(pl.BlockSpec((tm, aw), lambda i: (i, 0)))
    out_shape += r_shapes
    out_specs += r_specs
    scratch = [pltpu.VMEM((1, n_e), F32)]
    if chunked:
        scratch.append(pltpu.VMEM((tm, aw), F32))
    outs = pl.pallas_call(
        functools.partial(_mixer_a_kernel, chunked=chunked, want_v=want_v),
        grid=(t // tm,), in_specs=in_specs, out_specs=out_specs, out_shape=out_shape,
        scratch_shapes=scratch, compiler_params=_cparams(1), name="mixer_a_router",
    )(*args)
    outs = list(outs)
    x1, xn = outs[0], outs[1]
    v = outs[2] if want_v else None
    tope, gates, rank, cnt = outs[-4:]
    return x1, xn, v, tope, gates, rank, cnt


def _route_plan(routes, tb):
    n_e = routes[0][2].shape[1]
    n_tok = sum(r[0].shape[0] for r in routes)
    counts = sum(r[2][0] for r in routes)
    pc = (counts + tb - 1) // tb * tb
    pend = jnp.cumsum(pc)
    base = pend - pc
    dests = []
    for tope, rank, cnt in routes:
        dests.append((base[tope] + rank).astype(jnp.int32))
        base = base + cnt[0]
    n_blocks = -(-(n_tok * TOP_K) // tb) + n_e
    blk_start = jnp.arange(n_blocks, dtype=jnp.int32) * tb
    blk_e = jnp.minimum(jnp.sum(pend[None, :] <= blk_start[:, None], axis=1), n_e - 1).astype(jnp.int32)
    n_used = (pend[-1] // tb).astype(jnp.int32)[None]
    return dests, blk_e, n_used, n_blocks


def _dispatch_kernel(dest_ref, xn_ref, xs_in_hbm, xs_hbm, sem, *, td):
    del xs_in_hbm

    def issue(t, carry):
        for k in range(TOP_K):
            d = dest_ref[0, 0, t * TOP_K + k]
            pltpu.make_async_copy(xn_ref.at[pl.ds(t, 1)], xs_hbm.at[pl.ds(d, 1)], sem).start(priority=k % 2)
        return carry
    lax.fori_loop(0, td, issue, 0)
    for _ in range(TOP_K):
        pltpu.make_async_copy(xn_ref, xs_hbm.at[pl.ds(0, td)], sem).wait()


def _dispatch(xn, dest, rows, td):
    t, d = xn.shape
    dest3 = dest.reshape(t // td, 1, td * TOP_K)
    return pl.pallas_call(
        functools.partial(_dispatch_kernel, td=td),
        grid=(t // td,),
        in_specs=[pl.BlockSpec((1, 1, td * TOP_K), lambda i: (i, 0, 0), memory_space=pltpu.SMEM),
                  pl.BlockSpec((td, d), lambda i: (i, 0)), pl.BlockSpec(memory_space=pl.ANY)],
        out_specs=pl.BlockSpec(memory_space=pl.ANY),
        out_shape=jax.ShapeDtypeStruct(rows.shape, F32),
        scratch_shapes=[pltpu.SemaphoreType.DMA(())],
        input_output_aliases={2: 0},
        compiler_params=_cparams(1), name="moe_dispatch",
    )(dest3, xn, rows)


def _experts_kernel(be_ref, nu_ref, xs_ref, wgu_ref, bgu_ref, wdn_ref, bdn_ref, y_ref, wgu_bf, wdn_bf):
    i = pl.program_id(0)
    live = i < nu_ref[0]
    f = wdn_ref.shape[2]
    changed = jnp.logical_or(i == 0, be_ref[i] != be_ref[jnp.maximum(i - 1, 0)])

    @pl.when(jnp.logical_and(live, changed))
    def _():
        wgu_bf[...] = wgu_ref[0, 0].astype(BF16)
        wdn_bf[...] = wdn_ref[0, 0].astype(BF16)

    @pl.when(live)
    def _():
        x = xs_ref[...]
        hgu = _mm(x, wgu_bf[...]) + bgu_ref[0, 0]
        gate = jnp.minimum(hgu[:, :f], SWIGLU_LIMIT)
        up = jnp.clip(hgu[:, f:], -SWIGLU_LIMIT, SWIGLU_LIMIT)
        act = gate * jax.nn.sigmoid(SWIGLU_ALPHA * gate) * (up + 1.0)
        y_ref[...] = _mm(act, wdn_bf[...]) + bdn_ref[0, 0]

    @pl.when(jnp.logical_not(live))
    def _():
        y_ref[...] = jnp.zeros_like(y_ref)


def _experts(xs, blk_e, n_used, n_blocks, p, l, tb):
    n_rows, d = xs.shape
    wgu, wdn = p['moe_w_gu'], p['moe_w_down']
    _, n_e, _, f2 = wgu.shape
    f = f2 // 2
    bgu = p['moe_b_gu'][:, :, None, :]
    bdn = p['moe_b_down'][:, :, None, :]

    def row_map(i, be, nu):
        return (jnp.minimum(i, nu[0] - 1), 0)

    def w_map(i, be, nu):
        return (l, be[jnp.minimum(i, nu[0] - 1)], 0, 0)

    return pl.pallas_call(
        _experts_kernel,
        grid_spec=pltpu.PrefetchScalarGridSpec(
            num_scalar_prefetch=2, grid=(n_blocks,),
            in_specs=[pl.BlockSpec((tb, d), row_map), pl.BlockSpec((1, 1, d, f2), w_map),
                      pl.BlockSpec((1, 1, 1, f2), w_map), pl.BlockSpec((1, 1, f, d), w_map),
                      pl.BlockSpec((1, 1, 1, d), w_map)],
            out_specs=pl.BlockSpec((tb, d), lambda i, be, nu: (i, 0)),
            scratch_shapes=[pltpu.VMEM((d, f2), BF16), pltpu.VMEM((f, d), BF16)]),
        out_shape=jax.ShapeDtypeStruct((n_rows, d), F32),
        compiler_params=_cparams(1), name="moe_experts",
    )(blk_e, n_used, xs, wgu, bgu, wdn, bdn)


def _moe_rows(groups, p, l, spare_rows=None):
    tb = PROMPT_EXPERT_BLOCK
    dests, blk_e, n_used, n_blocks = _route_plan([g[1:4] for g in groups], tb)
    xs = jnp.zeros((n_blocks * tb, groups[0][0].shape[1]), F32) if spare_rows is None else spare_rows
    for (xn, _, _, _, td), dest in zip(groups, dests):
        xs = _dispatch(xn, dest, xs, td)
    return _experts(xs, blk_e, n_used, n_blocks, p, l, tb), dests, xs


def _combine_kvq_kernel(dest_ref, x1_ref, gate_ref, y_hbm, gkv_ref, wk_ref, wv_ref, gmix_ref, wq_ref,
                        x2_ref, k_ref, v_ref, *rest, for_attn):
    if for_attn:
        qt_ref, kb_ref, vt_ref, ybuf, sem = rest
    else:
        q_ref, ybuf, sem = rest
    tm = x1_ref.shape[0]
    x2 = x1_ref[...] + _gather_combine(dest_ref, y_hbm, ybuf, sem, gate_ref[...], tm)
    x2_ref[...] = x2
    hk = _rms(x2, gkv_ref[...])
    k = _mm(hk, wk_ref[...])
    v = _mm(hk, wv_ref[...])
    k_ref[...] = k
    v_ref[...] = v
    q = _mm(_rms(x2, gmix_ref[...]), wq_ref[...])
    if for_attn:
        n_h = vt_ref.shape[0]
        qt_ref[:, 0] = (q * (ATTN_SCALE * LOG2E)).T.reshape(n_h, V7X_LANES, tm).astype(BF16)
        kb_ref[...] = k.astype(BF16)
        vt_ref[:, 0, :V_DIM] = v.T.reshape(n_h, V_DIM, tm).astype(BF16)
        ones_row = lax.broadcasted_iota(jnp.int32, (n_h, VT_PAD_ROWS, tm), 1) == 0
        vt_ref[:, 0, V_DIM:] = jnp.where(ones_row, 1.0, 0.0).astype(BF16)
    else:
        q_ref[...] = q


def _combine_kvq(x1, gates, y_rows, dest, p, *, for_attn, tm):
    t, d = x1.shape
    dest3 = dest.reshape(t // tm, 1, tm * TOP_K)
    args = [dest3, x1, gates, y_rows, p['g_kv'][None, :], p['w_k'].astype(BF16), p['w_v'].astype(BF16),
            p['g_mix'][N_A_LAYERS][None, :], p['b_w_q'][0].astype(BF16)]
    tile = pl.BlockSpec((tm, d), lambda i: (i, 0))
    in_specs = [pl.BlockSpec((1, 1, tm * TOP_K), lambda i: (i, 0, 0), memory_space=pltpu.SMEM),
                tile, pl.BlockSpec((tm, TOP_K), lambda i: (i, 0)), pl.BlockSpec(memory_space=pl.ANY)]
    in_specs += [_full(a.shape) for a in args[4:]]
    out_shape = [jax.ShapeDtypeStruct((t, d), F32)] * 3
    out_specs = [tile] * 3
    if for_attn:
        n_h = d // V7X_LANES
        out_shape += [jax.ShapeDtypeStruct((n_h, t // tm, V7X_LANES, tm), BF16), jax.ShapeDtypeStruct((t, d), BF16),
                      jax.ShapeDtypeStruct((n_h, t // tm, V_DIM + VT_PAD_ROWS, tm), BF16)]
        out_specs += [pl.BlockSpec((n_h, 1, V7X_LANES, tm), lambda i: (0, i, 0, 0)), tile,
                      pl.BlockSpec((n_h, 1, V_DIM + VT_PAD_ROWS, tm), lambda i: (0, i, 0, 0))]
    else:
        out_shape += [jax.ShapeDtypeStruct((t, d), F32)]
        out_specs += [tile]
    return pl.pallas_call(
        functools.partial(_combine_kvq_kernel, for_attn=for_attn),
        grid=(t // tm,), in_specs=in_specs, out_specs=out_specs, out_shape=out_shape,
        scratch_shapes=[pltpu.VMEM((TOP_K, tm, d), F32), pltpu.SemaphoreType.DMA(())],
        compiler_params=_cparams(1), name="moe_combine_kvq",
    )(*args)


def _bucket_np(n):
    n = np.asarray(n)
    nf = np.maximum(n, 1).astype(np.float32)
    large = MAX_EXACT + (np.log(nf / np.float32(MAX_EXACT)) / np.float32(math.log(MAX_DISTANCE / MAX_EXACT))
                         * np.float32(N_BUCKETS - MAX_EXACT)).astype(np.int32)
    return np.where(n < MAX_EXACT, n, np.minimum(large, N_BUCKETS - 1)).astype(np.int32)


def _bias_tiles_kernel(tbl_ref, bkt_ref, bkt_s_ref, tile_ref, last_ref, zero_ref):
    h = pl.program_id(0)
    far = tbl_ref[N_BUCKETS - 1, h]
    for dl in range(2):
        b = bkt_ref[dl]
        acc = jnp.where(b < 0, NEG_BIG, 0.0).astype(F32)
        for bb in range(N_BUCKETS - 1):
            acc = jnp.where(b == bb, (tbl_ref[bb, h] - far) * LOG2E, acc)
        tile_ref[0, dl] = acc

    @pl.when(h == 0)
    def _():
        last_ref[...] = jnp.zeros_like(last_ref)
        zero_ref[...] = jnp.zeros_like(zero_ref)

    bs = bkt_s_ref[...]
    val = jnp.zeros(bs.shape, F32)
    for bb in range(N_BUCKETS - 1):
        val = jnp.where(bs == bb, tbl_ref[bb, h] - far, val)
    row = lax.broadcasted_iota(jnp.int32, last_ref.shape, 0) % N_HEADS
    last_ref[...] = jnp.where(row == h, val, last_ref[...])
    zero_ref[...] = jnp.where(row == h, tbl_ref[0, h] - far, zero_ref[...])


def _bias_tiles(rel_table, tb, page):
    assert tb >= MAX_DISTANCE and page >= MAX_DISTANCE and page == V7X_LANES
    kq = np.arange(tb)
    tiles = []
    for dl in range(2):
        dist = dl * tb + kq[None, :] - kq[:, None]
        tiles.append(np.where(dist >= 0, _bucket_np(np.maximum(dist, 0)), -1))
    bkt = jnp.asarray(np.stack(tiles).astype(np.int32))
    bkt_s = jnp.asarray(_bucket_np(page - np.arange(page))[None, :].astype(np.int32))
    rows = (2 * N_HEADS, V7X_LANES)
    return pl.pallas_call(
        _bias_tiles_kernel,
        grid=(N_HEADS,),
        in_specs=[pl.BlockSpec(memory_space=pltpu.SMEM), _full(bkt.shape), _full(bkt_s.shape)],
        out_specs=[pl.BlockSpec((1, 2, tb, tb), lambda h: (h, 0, 0, 0)),
                   pl.BlockSpec(rows, lambda h: (0, 0)), pl.BlockSpec(rows, lambda h: (0, 0))],
        out_shape=[jax.ShapeDtypeStruct((N_HEADS, 2, tb, tb), F32),
                   jax.ShapeDtypeStruct(rows, F32), jax.ShapeDtypeStruct(rows, F32)],
        compiler_params=_cparams(1), name="rel_bias_tiles",
    )(rel_table, bkt, bkt_s)


def _attn_prompt_kernel(lam_ref, q_ref, k_ref, vt_ref, bias_ref, o_ref, acc_ref, m_ref, sa_ref, sb_ref,
                        pa_ref, pb_ref, *, tb):
    qi = pl.program_id(1)
    qt = q_ref[0, 0]
    row = lax.broadcasted_iota(jnp.int32, qt.shape, 0)
    zero = jnp.zeros_like(qt)
    qcat = jnp.concatenate([jnp.where(row < HEAD_DIM, qt, zero), jnp.where(row >= HEAD_DIM, qt, zero)], axis=1)
    acc_ref[...] = jnp.zeros_like(acc_ref)

    def scores(j, bias=None):
        k_blk = k_ref[pl.ds(pl.multiple_of(j * tb, tb), tb), :]
        s = jnp.dot(k_blk, qcat, preferred_element_type=F32)
        if bias is not None:
            s = s + jnp.concatenate([bias, bias], axis=1)
        return s, jnp.max(s, axis=0, keepdims=True)

    def absorb(s_ref, p_ref, s_max, j, m_old):
        m_new = jnp.maximum(m_old, s_max)
        for c in range(0, tb, ATTN_ROW_CHUNK):
            p_ref[c:c + ATTN_ROW_CHUNK, :] = jnp.exp2((s_ref[c:c + ATTN_ROW_CHUNK, :] - m_new).astype(BF16))
        acc_ref[...] = (jnp.exp2(m_old - m_new) * acc_ref[...]
                        + jnp.dot(vt_ref[0, j], p_ref[...], preferred_element_type=F32))
        return m_new

    n_far = jnp.maximum(qi - 1, 0)
    n_pairs = lax.shift_right_logical(n_far, 1)
    neg = jnp.full((1, 2 * tb), NEG_BIG, F32)
    m_ref[...] = neg

    @pl.when(n_pairs > 0)
    def _():
        sa_ref[...], m_ref[...] = scores(0)

    def pair(i, carry):
        m_old, max_a = carry
        a = 2 * i
        sb_ref[...], max_b = scores(a + 1)
        m_old = absorb(sa_ref, pa_ref, max_a, a, m_old)
        sa_ref[...], max_a = scores(a + 2)
        m_old = absorb(sb_ref, pb_ref, max_b, a + 1, m_old)
        return m_old, max_a

    m_ref[...], _ = lax.fori_loop(0, n_pairs, pair, (neg, m_ref[...]))

    def single(j, bias=None):
        sa_ref[...], s_max = scores(j, bias)
        m_ref[...] = absorb(sa_ref, pa_ref, s_max, j, m_ref[...])

    @pl.when(n_far > 2 * n_pairs)
    def _():
        single(2 * n_pairs)

    @pl.when(qi >= 1)
    def _():
        sa_ref[...], max_a = scores(qi - 1, bias_ref[0, 1])
        sb_ref[...], max_b = scores(qi, bias_ref[0, 0])
        m_mid = absorb(sa_ref, pa_ref, max_a, qi - 1, m_ref[...])
        absorb(sb_ref, pb_ref, max_b, qi, m_mid)

    @pl.when(qi == 0)
    def _():
        single(0, bias_ref[0, 0])
    o_all = acc_ref[:V_DIM, :] / acc_ref[V_DIM:V_DIM + 1, :]
    o_ref[...] = (o_all[:, :tb] - lam_ref[0] * o_all[:, tb:]).T


def _attn_prompt(qt4, kb, vt4, bias_tiles, lam, tb):
    t, d = kb.shape
    n_h = d // V7X_LANES
    n_q = t // tb
    vt_rows = vt4.shape[2]
    return pl.pallas_call(
        functools.partial(_attn_prompt_kernel, tb=tb),
        grid=(n_h, n_q),
        in_specs=[pl.BlockSpec(memory_space=pltpu.SMEM),
                  pl.BlockSpec((1, 1, V7X_LANES, tb), lambda h, i: (h, i, 0, 0)),
                  pl.BlockSpec((t, V7X_LANES), lambda h, i: (0, h)),
                  pl.BlockSpec((1, n_q, vt_rows, tb), lambda h, i: (h, 0, 0, 0)),
                  pl.BlockSpec((1, 2, tb, tb), lambda h, i: (h, 0, 0, 0))],
        out_specs=pl.BlockSpec((tb, V7X_LANES), lambda h, i: (i, h)),
        out_shape=jax.ShapeDtypeStruct((t, d), F32),
        scratch_shapes=[pltpu.VMEM((vt_rows, 2 * tb), F32), pltpu.VMEM((1, 2 * tb), F32),
                        pltpu.VMEM((tb, 2 * tb), F32), pltpu.VMEM((tb, 2 * tb), F32),
                        pltpu.VMEM((tb, 2 * tb), BF16), pltpu.VMEM((tb, 2 * tb), BF16)],
        compiler_params=_cparams(2), name="attn_prompt",
    )(lam, qt4, kb, vt4, bias_tiles)


def _attn_sample_kernel(pt_ref, lam_ref, qbd_ref, q_ref, kn_ref, vn_ref, hmask_ref, expand_ref, blast_ref,
                        bzero_ref, *rest, n_pp):
    k_refs = rest[:n_pp]
    v_refs = rest[n_pp:2 * n_pp]
    o_ref, acc_ref, m_ref, l_ref = rest[2 * n_pp:]
    del pt_ref
    g = pl.program_id(1)
    n_g = pl.num_programs(1)
    n_h = N_HEADS

    @pl.when(g == 0)
    def _():
        qv = q_ref[0].astype(BF16).astype(F32)
        s0 = jnp.sum(qv * kn_ref[0].astype(BF16).astype(F32), axis=-1, keepdims=True)
        m_ref[...] = s0 + bzero_ref[:, 0:1]
        l_ref[...] = jnp.ones_like(l_ref)
        vn = vn_ref[0].astype(BF16).astype(F32)
        acc_ref[...] = jnp.concatenate([vn, vn], axis=0)

    qbd = qbd_ref[0].astype(BF16)
    parts = []
    for i in range(n_pp):
        kst = k_refs[i][0]
        kst = kst.reshape(kst.shape[0] * kst.shape[1], kst.shape[2]).astype(BF16)
        parts.append(jnp.dot(qbd, kst, preferred_element_type=F32))
    parts[-1] = parts[-1] + jnp.where(g == n_g - 1, 1.0, 0.0) * blast_ref[...]
    s = jnp.concatenate(parts, axis=1)
    m_old = m_ref[...]
    m_new = jnp.maximum(m_old, jnp.max(s, axis=-1, keepdims=True))
    alpha = jnp.exp(m_old - m_new)
    pr = jnp.exp(s - m_new)
    l_ref[...] = alpha * l_ref[...] + jnp.sum(pr, axis=-1, keepdims=True)
    m_ref[...] = m_new
    keys = pr.shape[1] // n_pp
    n_m = pr.shape[0]
    p_rows = jnp.concatenate([pr[:, i * keys:(i + 1) * keys] for i in range(n_pp)], axis=0).astype(BF16)
    spread = jnp.dot(p_rows, expand_ref[...], preferred_element_type=F32) * hmask_ref[...]
    spread = spread.astype(BF16)
    acc = alpha * acc_ref[...]
    for i in range(n_pp):
        acc = acc + jnp.dot(spread[i * n_m:(i + 1) * n_m], v_refs[i][0].astype(BF16),
                            preferred_element_type=F32)
    acc_ref[...] = acc

    @pl.when(g == n_g - 1)
    def _():
        o = acc_ref[...] / l_ref[...]
        o_ref[0] = o[:n_h] - lam_ref[0] * o[n_h:]


def _attn_sample(q, k_new, v_new, cache_k, cache_v, page_table, blast, bzero, lam):
    nb, d = q.shape
    n_phys, page = cache_k.shape[0], cache_k.shape[1]
    n_pages = page_table.shape[1]
    n_pp = math.gcd(PAGES_PER_STEP, n_pages)
    n_h = N_HEADS
    n_m = 2 * n_h
    perm = np.concatenate([np.arange(0, n_m, 2), np.arange(1, n_m, 2)])
    qp = (q * ATTN_SCALE).reshape(nb, n_m, HEAD_DIM)[:, perm]
    kn = k_new.reshape(nb, n_m, HEAD_DIM)[:, perm]
    vn = v_new.reshape(nb, n_h, V_DIM)
    own = jnp.asarray((perm[:, None] == np.arange(n_m)[None, :])[None, :, :, None])
    qbd = jnp.where(own, qp[:, :, None, :], 0.0).reshape(nb, n_m, n_m * HEAD_DIM)
    ckt = jnp.transpose(cache_k, (0, 2, 3, 1))
    cv = cache_v.reshape(n_phys, page * n_h, V_DIM)
    lanes = np.arange(page * n_h)
    hmask = jnp.asarray((lanes[None, :] % n_h == np.arange(n_pp * n_m)[:, None] % n_h).astype(np.float32))
    expand = jnp.asarray((lanes[None, :] // n_h == np.arange(page)[:, None]).astype(np.float32), dtype=BF16)

    def page_map(i):
        return lambda b, g, pt: (pt[b * n_pages + g * n_pp + i], 0, 0, 0)

    def page_map3(i):
        return lambda b, g, pt: (pt[b * n_pages + g * n_pp + i], 0, 0)
    row3 = lambda b, g, pt: (b, 0, 0)
    const2 = lambda b, g, pt: (0, 0)
    in_specs = [pl.BlockSpec(memory_space=pltpu.SMEM),
                pl.BlockSpec((1, n_m, n_m * HEAD_DIM), row3),
                pl.BlockSpec((1, n_m, HEAD_DIM), row3), pl.BlockSpec((1, n_m, HEAD_DIM), row3),
                pl.BlockSpec((1, n_h, V_DIM), row3), pl.BlockSpec(hmask.shape, const2),
                pl.BlockSpec(expand.shape, const2), pl.BlockSpec(blast.shape, const2),
                pl.BlockSpec(bzero.shape, const2)]
    in_specs += [pl.BlockSpec((1, n_m, HEAD_DIM, page), page_map(i)) for i in range(n_pp)]
    in_specs += [pl.BlockSpec((1, page * n_h, V_DIM), page_map3(i)) for i in range(n_pp)]
    out = pl.pallas_call(
        functools.partial(_attn_sample_kernel, n_pp=n_pp),
        grid_spec=pltpu.PrefetchScalarGridSpec(
            num_scalar_prefetch=1, grid=(nb, n_pages // n_pp), in_specs=in_specs,
            out_specs=pl.BlockSpec((1, n_h, V_DIM), row3),
            scratch_shapes=[pltpu.VMEM((n_m, V_DIM), F32), pltpu.VMEM((n_m, 1), F32),
                            pltpu.VMEM((n_m, 1), F32)]),
        out_shape=jax.ShapeDtypeStruct((nb, n_h, V_DIM), F32),
        compiler_params=_cparams(2), name="attn_sample",
    )(page_table.reshape(-1).astype(jnp.int32), lam, qbd, qp, kn, vn, hmask, expand, blast, bzero,
      *([ckt] * n_pp), *([cv] * n_pp))
    return out.reshape(nb, d)


def _attn_out_kernel(o_ref, x2_ref, gsub_ref, wo_ref, gffn_ref, wr_ref, br_ref,
                     x3_ref, xn_ref, tope_ref, gate_ref, rank_ref, cnt_ref, runcnt_ref, on_ref,
                     *, sub_scale):
    @pl.when(pl.program_id(0) == 0)
    def _():
        runcnt_ref[...] = jnp.zeros_like(runcnt_ref)

    vd = gsub_ref.shape[1]
    for h in range(o_ref.shape[1] // vd):
        oh = o_ref[:, h * vd:(h + 1) * vd]
        on_ref[:, h * vd:(h + 1) * vd] = _rms(oh, gsub_ref[...]) * sub_scale
    x3 = x2_ref[...] + _mm(on_ref[...], wo_ref[...])
    x3_ref[...] = x3
    _router_tail(x3, gffn_ref[...], wr_ref[...], br_ref[...], runcnt_ref,
                 xn_ref, tope_ref, gate_ref, rank_ref, cnt_ref)


def _attn_out(o, x2, p, l, sub_scale, *, tm):
    t, d = x2.shape
    n_e = p['moe_w_router'].shape[-1]
    args = [o, x2, p['b_g_sub'][0][None, :], p['b_w_o'][0].astype(BF16), p['g_ffn'][l][None, :],
            p['moe_w_router'][l].astype(BF16), p['moe_b_router'][l][None, :]]
    tile = pl.BlockSpec((tm, d), lambda i: (i, 0))
    r_shapes, r_specs = _router_out_specs(t, tm, d, n_e)
    outs = pl.pallas_call(
        functools.partial(_attn_out_kernel, sub_scale=sub_scale),
        grid=(t // tm,), in_specs=[tile, tile] + [_full(a.shape) for a in args[2:]],
        out_specs=[tile, tile] + r_specs,
        out_shape=[jax.ShapeDtypeStruct((t, d), F32)] * 2 + r_shapes,
        scratch_shapes=[pltpu.VMEM((1, n_e), F32), pltpu.VMEM((tm, d), F32)],
        compiler_params=_cparams(1), name="attn_out_router",
    )(*args)
    return outs


def _combine_final_kernel(dest_ref, x3_ref, gate_ref, y_hbm, gout_ref, y_ref, ybuf, sem):
    tm = x3_ref.shape[0]
    x4 = x3_ref[...] + _gather_combine(dest_ref, y_hbm, ybuf, sem, gate_ref[...], tm)
    y_ref[...] = _rms(x4, gout_ref[...])


def _combine_final(x3, gates, y_rows, dest, g_out, tm):
    t, d = x3.shape
    dest3 = dest.reshape(t // tm, 1, tm * TOP_K)
    tile = pl.BlockSpec((tm, d), lambda i: (i, 0))
    return pl.pallas_call(
        _combine_final_kernel,
        grid=(t // tm,),
        in_specs=[pl.BlockSpec((1, 1, tm * TOP_K), lambda i: (i, 0, 0), memory_space=pltpu.SMEM),
                  tile, pl.BlockSpec((tm, TOP_K), lambda i: (i, 0)), pl.BlockSpec(memory_space=pl.ANY),
                  _full((1, d))],
        out_specs=tile, out_shape=jax.ShapeDtypeStruct((t, d), F32),
        scratch_shapes=[pltpu.VMEM((TOP_K, tm, d), F32), pltpu.SemaphoreType.DMA(())],
        compiler_params=_cparams(1), name="moe_combine_final",
    )(dest3, x3, gates, y_rows, g_out[None, :])


def kernel(x_prompt, x_sample, cache_k, cache_v, page_table, g_mix, g_ffn, g_kv, g_out, a_w_in, a_b_in, a_g_v, a_b_v, a_w_s, a_b_s, a_w_out, w_k, w_v, b_w_q, b_lq1, b_lk1, b_lq2, b_lk2, b_g_sub, b_w_o, rel_table, moe_w_router, moe_b_router, moe_w_gu, moe_b_gu, moe_w_down, moe_b_down):
    p = {
        'g_mix': g_mix, 'g_ffn': g_ffn, 'g_kv': g_kv, 'g_out': g_out,
        'a_w_in': a_w_in, 'a_b_in': a_b_in, 'a_g_v': a_g_v, 'a_b_v': a_b_v,
        'a_w_s': a_w_s, 'a_b_s': a_b_s, 'a_w_out': a_w_out,
        'w_k': w_k, 'w_v': w_v, 'b_w_q': b_w_q, 'b_g_sub': b_g_sub, 'b_w_o': b_w_o,
        'moe_w_router': moe_w_router, 'moe_b_router': moe_b_router,
        'moe_w_gu': moe_w_gu, 'moe_b_gu': moe_b_gu, 'moe_w_down': moe_w_down, 'moe_b_down': moe_b_down,
    }
    assert x_prompt.shape[0] == 1 and x_sample.shape[1] == 1
    b, s, d = x_prompt.shape
    nb = x_sample.shape[0]
    page = cache_k.shape[1]
    lam_init = 0.8 - 0.6 * math.exp(-0.3 * N_A_LAYERS)
    lam = (jnp.exp(jnp.sum(b_lq1[0] * b_lk1[0])) - jnp.exp(jnp.sum(b_lq2[0] * b_lk2[0])) + lam_init)
    lam = lam.astype(F32)[None]
    tb = min(ATTN_BLOCK, s)
    bias_tiles, blast, bzero = _bias_tiles(rel_table, tb, page)
    tm_p, tm_s = PROMPT_TOKEN_TILE, nb
    assert s % tm_p == 0 and tm_p % CHUNK == 0 and s % tb == 0

    x1_p, xn_p, _, tope_p, gates_p, rank_p, cnt_p = _mixer_a(
        x_prompt[0], p, 0, chunked=True, want_v=False, tm=tm_p)
    x1_s, xn_s, cv_s, tope_s, gates_s, rank_s, cnt_s = _mixer_a(
        x_sample[:, 0], p, 0, chunked=False, want_v=True, tm=tm_s)
    y_rows, (dest_p, dest_s), rows_buf = _moe_rows(
        [(xn_p, tope_p, rank_p, cnt_p, tm_p), (xn_s, tope_s, rank_s, cnt_s, tm_s)], p, 0)
    x2_p, k_p, v_p, qt4, kb, vt4 = _combine_kvq(x1_p, gates_p, y_rows, dest_p, p, for_attn=True, tm=tb)
    x2_s, k_s, v_s, q_s = _combine_kvq(x1_s, gates_s, y_rows, dest_s, p, for_attn=False, tm=tm_s)

    o_p = _attn_prompt(qt4, kb, vt4, bias_tiles, lam, tb)
    o_s = _attn_sample(q_s, k_s, v_s, cache_k, cache_v, page_table, blast, bzero, lam)
    x3_p, xn_p, tope_p, gates_p, rank_p, cnt_p = _attn_out(o_p, x2_p, p, 1, 1.0 - lam_init, tm=tm_p)
    x3_s, xn_s, tope_s, gates_s, rank_s, cnt_s = _attn_out(o_s, x2_s, p, 1, 1.0 - lam_init, tm=tm_s)
    y_rows, (dest_p, dest_s), _ = _moe_rows(
        [(xn_p, tope_p, rank_p, cnt_p, tm_p), (xn_s, tope_s, rank_s, cnt_s, tm_s)], p, 1, rows_buf)
    y_p = _combine_final(x3_p, gates_p, y_rows, dest_p, g_out, tm_p)
    y_s = _combine_final(x3_s, gates_s, y_rows, dest_s, g_out, tm_s)
    n_h = N_HEADS
    return (y_p[None], y_s[:, None],
            k_p.reshape(1, s, 2 * n_h, HEAD_DIM), v_p.reshape(1, s, n_h, V_DIM),
            k_s.reshape(nb, 1, 2 * n_h, HEAD_DIM), v_s.reshape(nb, 1, n_h, V_DIM),
            cv_s.reshape(1, nb, 1, -1))
```

```python
import functools
import math

import numpy as np
import jax
import jax.numpy as jnp
from jax import lax
from jax.experimental import pallas as pl
from jax.experimental.pallas import tpu as pltpu

F32 = jnp.float32
BF16 = jnp.bfloat16

EPS = 1e-6
CHUNK = 128
A_GROUPS = 8
N_HEADS = 8
HEAD_DIM = 64
V_DIM = 2 * HEAD_DIM
ATTN_SCALE = HEAD_DIM ** -0.5
N_BUCKETS = 32
MAX_EXACT = N_BUCKETS // 2
MAX_DISTANCE = 128
TOP_K = 4
SWIGLU_LIMIT = 7.0
SWIGLU_ALPHA = 1.702
N_A_LAYERS = 1

V7X_LANES = 128
V7X_SUBLANES = 8
V7X_VMEM_LIMIT_BYTES = 56 * 1024 * 1024

NEG_BIG = -1e30
PROMPT_TOKEN_TILE = 512
PROMPT_EXPERT_BLOCK = 512
ATTN_BLOCK = 512
ATTN_ROW_CHUNK = 64
VT_PAD_ROWS = 16
LOG2E = math.log2(math.e)
PAGES_PER_STEP = 8


def _cparams(n_axes):
    return pltpu.CompilerParams(
        dimension_semantics=("arbitrary",) * n_axes,
        vmem_limit_bytes=V7X_VMEM_LIMIT_BYTES,
    )


def _mm(a, w):
    return jnp.dot(a.astype(BF16), w.astype(BF16), preferred_element_type=F32)


def _rms(x, g):
    return x * lax.rsqrt(jnp.mean(x * x, axis=-1, keepdims=True) + EPS) * g


def _gelu(x):
    return 0.5 * x * (1.0 + lax.erf(x * (2.0 ** -0.5)))


def _cols4(c0, c1, c2, c3):
    m = c0.shape[0]
    lane = lax.broadcasted_iota(jnp.int32, (m, TOP_K), 1)
    return jnp.where(lane == 0, c0, jnp.where(lane == 1, c1, jnp.where(lane == 2, c2, c3)))


def _router_tail(x1, gffn, wr, br, runcnt_ref, xn_ref, tope_ref, gate_ref, rank_ref, cnt_ref):
    m = x1.shape[0]
    xn = _rms(x1, gffn)
    xn_ref[...] = xn
    logits = _mm(xn, wr) + br
    n_e = logits.shape[1]
    lane = lax.broadcasted_iota(jnp.int32, (m, n_e), 1).astype(F32)
    vals, sels, ohs = [], [], []
    cur = logits
    for _ in range(TOP_K):
        mx = jnp.max(cur, axis=-1, keepdims=True)
        sel = jnp.min(jnp.where(cur == mx, lane, float(n_e)), axis=-1, keepdims=True)
        oh = lane == sel
        vals.append(mx)
        sels.append(sel)
        ohs.append(oh.astype(F32))
        cur = jnp.where(oh, -jnp.inf, cur)
    es = [jnp.exp(v - vals[0]) for v in vals]
    den = es[0] + es[1] + es[2] + es[3]
    gate_ref[...] = _cols4(*[e / den for e in es])
    tope_ref[...] = _cols4(*sels).astype(jnp.int32)
    row = lax.broadcasted_iota(jnp.int32, (m, m), 0)
    col = lax.broadcasted_iota(jnp.int32, (m, m), 1)
    ltri = (row > col).astype(BF16)
    prior = runcnt_ref[...]
    ranks = []
    for oh in ohs:
        within = jnp.dot(ltri, oh.astype(BF16), preferred_element_type=F32)
        ranks.append(jnp.sum((prior + within) * oh, axis=-1, keepdims=True))
        prior = prior + jnp.sum(oh, axis=0, keepdims=True)
    rank_ref[...] = _cols4(*ranks).astype(jnp.int32)
    runcnt_ref[...] = prior
    cnt_ref[...] = prior.astype(jnp.int32)


def _gather_combine(dest_ref, y_hbm, ybuf, sem, gates, tm):
    def issue(t, carry):
        for k in range(TOP_K):
            d = dest_ref[0, 0, t * TOP_K + k]
            pltpu.make_async_copy(y_hbm.at[pl.ds(d, 1)], ybuf.at[k, pl.ds(t, 1)], sem).start(priority=k % 2)
        return carry
    lax.fori_loop(0, tm, issue, 0)
    for k in range(TOP_K):
        pltpu.make_async_copy(y_hbm.at[pl.ds(0, tm)], ybuf.at[k], sem).wait()
    acc = gates[:, 0:1] * ybuf[0]
    for k in range(1, TOP_K):
        acc = acc + gates[:, k:k + 1] * ybuf[k]
    return acc


def _mixer_a_kernel(*refs, chunked, want_v):
    (x_ref, gmix_ref, win_ref, bin_ref, gv_ref, bv_ref, ws_ref, bs_ref, wout_ref,
     gffn_ref, wr_ref, br_ref) = refs[:12]
    rest = list(refs[12:])
    x1_ref, xn_ref = rest[0], rest[1]
    rest = rest[2:]
    v_ref = rest.pop(0) if want_v else None
    tope_ref, gate_ref, rank_ref, cnt_ref, runcnt_ref = rest[:5]
    us_ref = rest[5] if chunked else None

    @pl.when(pl.program_id(0) == 0)
    def _():
        runcnt_ref[...] = jnp.zeros_like(runcnt_ref)

    x = x_ref[...]
    tm = x.shape[0]
    aw = wout_ref.shape[0]
    h = _rms(x, gmix_ref[...])
    u = _gelu(_mm(h, win_ref[:, :aw]) + bin_ref[:, :aw])
    vr = _gelu(_mm(h, win_ref[:, aw:]) + bin_ref[:, aw:])
    mu = jnp.mean(vr, axis=-1, keepdims=True)
    vc = vr - mu
    v = vc * lax.rsqrt(jnp.mean(vc * vc, axis=-1, keepdims=True) + EPS) * gv_ref[...] + bv_ref[...]
    if want_v:
        v_ref[...] = v
    if chunked:
        n_g, cl = ws_ref.shape[0], ws_ref.shape[1]
        gd = aw // n_g
        tri = (lax.broadcasted_iota(jnp.int32, (cl, cl), 0) >= lax.broadcasted_iota(jnp.int32, (cl, cl), 1))
        for g in range(n_g):
            wm = jnp.where(tri, ws_ref[g], 0.0).astype(BF16)
            for c in range(tm // cl):
                vg = v[c * cl:(c + 1) * cl, g * gd:(g + 1) * gd].astype(BF16)
                s = jnp.dot(wm, vg, preferred_element_type=F32) + bs_ref[g]
                us_ref[c * cl:(c + 1) * cl, g * gd:(g + 1) * gd] = u[c * cl:(c + 1) * cl, g * gd:(g + 1) * gd] * s
        us = us_ref[...]
    else:
        us = u * (v * ws_ref[...] + bs_ref[...])
    x1 = x + _mm(us, wout_ref[...])
    x1_ref[...] = x1
    _router_tail(x1, gffn_ref[...], wr_ref[...], br_ref[...], runcnt_ref,
                 xn_ref, tope_ref, gate_ref, rank_ref, cnt_ref)


def _full(shape):
    nd = len(shape)
    return pl.BlockSpec(shape, lambda *_: (0,) * nd, pipeline_mode=pl.Buffered(1))


def _router_out_specs(t, tm, d, n_e):
    shapes = [jax.ShapeDtypeStruct((t, TOP_K), jnp.int32), jax.ShapeDtypeStruct((t, TOP_K), F32),
              jax.ShapeDtypeStruct((t, TOP_K), jnp.int32), jax.ShapeDtypeStruct((1, n_e), jnp.int32)]
    specs = [pl.BlockSpec((tm, TOP_K), lambda i: (i, 0)), pl.BlockSpec((tm, TOP_K), lambda i: (i, 0)),
             pl.BlockSpec((tm, TOP_K), lambda i: (i, 0)), pl.BlockSpec((1, n_e), lambda i: (0, 0))]
    return shapes, specs


def _mixer_a(x, p, l, *, chunked, want_v, tm):
    t, d = x.shape
    w_in = p['a_w_in'][l].astype(BF16)
    w_out = p['a_w_out'][l].astype(BF16)
    aw = w_out.shape[0]
    n_e = p['moe_w_router'].shape[-1]
    if chunked:
        ws = p['a_w_s'][l]
        bs = p['a_b_s'][l][:, :, None]
    else:
        gd = aw // A_GROUPS
        ws = jnp.repeat(p['a_w_s'][l][:, 0, 0], gd)[None, :]
        bs = jnp.repeat(p['a_b_s'][l][:, 0], gd)[None, :]
    args = [x, p['g_mix'][l][None, :], w_in, p['a_b_in'][l][None, :], p['a_g_v'][l][None, :],
            p['a_b_v'][l][None, :], ws, bs, w_out, p['g_ffn'][l][None, :],
            p['moe_w_router'][l].astype(BF16), p['moe_b_router'][l][None, :]]
    in_specs = [pl.BlockSpec((tm, d), lambda i: (i, 0))] + [_full(a.shape) for a in args[1:]]
    r_shapes, r_specs = _router_out_specs(t, tm, d, n_e)
    out_shape = [jax.ShapeDtypeStruct((t, d), F32), jax.ShapeDtypeStruct((t, d), F32)]
    out_specs = [pl.BlockSpec((tm, d), lambda i: (i, 0)), pl.BlockSpec((tm, d), lambda i: (i, 0))]
    if want_v:
        out_shape.append(jax.ShapeDtypeStruct((t, aw), F32))
        out_specs.append(pl.BlockSpec((tm, aw), lambda i: (i, 0)))
    out_shape += r_shapes
    out_specs += r_specs
    scratch = [pltpu.VMEM((1, n_e), F32)]
    if chunked:
        scratch.append(pltpu.VMEM((tm, aw), F32))
    outs = pl.pallas_call(
        functools.partial(_mixer_a_kernel, chunked=chunked, want_v=want_v),
        grid=(t // tm,), in_specs=in_specs, out_specs=out_specs, out_shape=out_shape,
        scratch_shapes=scratch, compiler_params=_cparams(1), name="mixer_a_router",
    )(*args)
    outs = list(outs)
    x1, xn = outs[0], outs[1]
    v = outs[2] if want_v else None
    tope, gates, rank, cnt = outs[-4:]
    return x1, xn, v, tope, gates, rank, cnt


def _route_plan(routes, tb):
    n_e = routes[0][2].shape[1]
    n_tok = sum(r[0].shape[0] for r in routes)
    counts = sum(r[2][0] for r in routes)
    pc = (counts + tb - 1) // tb * tb
    pend = jnp.cumsum(pc)
    base = pend - pc
    dests = []
    for tope, rank, cnt in routes:
        dests.append((base[tope] + rank).astype(jnp.int32))
        base = base + cnt[0]
    n_blocks = -(-(n_tok * TOP_K) // tb) + n_e
    blk_start = jnp.arange(n_blocks, dtype=jnp.int32) * tb
    blk_e = jnp.minimum(jnp.sum(pend[None, :] <= blk_start[:, None], axis=1), n_e - 1).astype(jnp.int32)
    n_used = (pend[-1] // tb).astype(jnp.int32)[None]
    return dests, blk_e, n_used, n_blocks


def _dispatch_kernel(dest_ref, xn_ref, xs_in_hbm, xs_hbm, sem, *, td):
    del xs_in_hbm

    def issue(t, carry):
        for k in range(TOP_K):
            d = dest_ref[0, 0, t * TOP_K + k]
            pltpu.make_async_copy(xn_ref.at[pl.ds(t, 1)], xs_hbm.at[pl.ds(d, 1)], sem).start(priority=k % 2)
        return carry
    lax.fori_loop(0, td, issue, 0)
    for _ in range(TOP_K):
        pltpu.make_async_copy(xn_ref, xs_hbm.at[pl.ds(0, td)], sem).wait()


def _dispatch(xn, dest, rows, td):
    t, d = xn.shape
    dest3 = dest.reshape(t // td, 1, td * TOP_K)
    return pl.pallas_call(
        functools.partial(_dispatch_kernel, td=td),
        grid=(t // td,),
        in_specs=[pl.BlockSpec((1, 1, td * TOP_K), lambda i: (i, 0, 0), memory_space=pltpu.SMEM),
                  pl.BlockSpec((td, d), lambda i: (i, 0)), pl.BlockSpec(memory_space=pl.ANY)],
        out_specs=pl.BlockSpec(memory_space=pl.ANY),
        out_shape=jax.ShapeDtypeStruct(rows.shape, F32),
        scratch_shapes=[pltpu.SemaphoreType.DMA(())],
        input_output_aliases={2: 0},
        compiler_params=_cparams(1), name="moe_dispatch",
    )(dest3, xn, rows)


def _experts_kernel(be_ref, nu_ref, xs_ref, wgu_ref, bgu_ref, wdn_ref, bdn_ref, y_ref, wgu_bf, wdn_bf):
    i = pl.program_id(0)
    live = i < nu_ref[0]
    f = wdn_ref.shape[2]
    changed = jnp.logical_or(i == 0, be_ref[i] != be_ref[jnp.maximum(i - 1, 0)])

    @pl.when(jnp.logical_and(live, changed))
    def _():
        wgu_bf[...] = wgu_ref[0, 0].astype(BF16)
        wdn_bf[...] = wdn_ref[0, 0].astype(BF16)

    @pl.when(live)
    def _():
        x = xs_ref[...]
        hgu = _mm(x, wgu_bf[...]) + bgu_ref[0, 0]
        gate = jnp.minimum(hgu[:, :f], SWIGLU_LIMIT)
        up = jnp.clip(hgu[:, f:], -SWIGLU_LIMIT, SWIGLU_LIMIT)
        act = gate * jax.nn.sigmoid(SWIGLU_ALPHA * gate) * (up + 1.0)
        y_ref[...] = _mm(act, wdn_bf[...]) + bdn_ref[0, 0]

    @pl.when(jnp.logical_not(live))
    def _():
        y_ref[...] = jnp.zeros_like(y_ref)


def _experts(xs, blk_e, n_used, n_blocks, p, l, tb):
    n_rows, d = xs.shape
    wgu, wdn = p['moe_w_gu'], p['moe_w_down']
    _, n_e, _, f2 = wgu.shape
    f = f2 // 2
    bgu = p['moe_b_gu'][:, :, None, :]
    bdn = p['moe_b_down'][:, :, None, :]

    def row_map(i, be, nu):
        return (jnp.minimum(i, nu[0] - 1), 0)

    def w_map(i, be, nu):
        return (l, be[jnp.minimum(i, nu[0] - 1)], 0, 0)

    return pl.pallas_call(
        _experts_kernel,
        grid_spec=pltpu.PrefetchScalarGridSpec(
            num_scalar_prefetch=2, grid=(n_blocks,),
            in_specs=[pl.BlockSpec((tb, d), row_map), pl.BlockSpec((1, 1, d, f2), w_map),
                      pl.BlockSpec((1, 1, 1, f2), w_map), pl.BlockSpec((1, 1, f, d), w_map),
                      pl.BlockSpec((1, 1, 1, d), w_map)],
            out_specs=pl.BlockSpec((tb, d), lambda i, be, nu: (i, 0)),
            scratch_shapes=[pltpu.VMEM((d, f2), BF16), pltpu.VMEM((f, d), BF16)]),
        out_shape=jax.ShapeDtypeStruct((n_rows, d), F32),
        compiler_params=_cparams(1), name="moe_experts",
    )(blk_e, n_used, xs, wgu, bgu, wdn, bdn)


def _moe_rows(groups, p, l, spare_rows=None):
    tb = PROMPT_EXPERT_BLOCK
    dests, blk_e, n_used, n_blocks = _route_plan([g[1:4] for g in groups], tb)
    xs = jnp.zeros((n_blocks * tb, groups[0][0].shape[1]), F32) if spare_rows is None else spare_rows
    for (xn, _, _, _, td), dest in zip(groups, dests):
        xs = _dispatch(xn, dest, xs, td)
    return _experts(xs, blk_e, n_used, n_blocks, p, l, tb), dests, xs


def _combine_kvq_kernel(dest_ref, x1_ref, gate_ref, y_hbm, gkv_ref, wk_ref, wv_ref, gmix_ref, wq_ref,
                        x2_ref, k_ref, v_ref, *rest, for_attn):
    if for_attn:
        qt_ref, kb_ref, vt_ref, ybuf, sem = rest
    else:
        q_ref, ybuf, sem = rest
    tm = x1_ref.shape[0]
    x2 = x1_ref[...] + _gather_combine(dest_ref, y_hbm, ybuf, sem, gate_ref[...], tm)
    x2_ref[...] = x2
    hk = _rms(x2, gkv_ref[...])
    k = _mm(hk, wk_ref[...])
    v = _mm(hk, wv_ref[...])
    k_ref[...] = k
    v_ref[...] = v
    q = _mm(_rms(x2, gmix_ref[...]), wq_ref[...])
    if for_attn:
        n_h = vt_ref.shape[0]
        qt_ref[:, 0] = (q * (ATTN_SCALE * LOG2E)).T.reshape(n_h, V7X_LANES, tm).astype(BF16)
        kb_ref[...] = k.astype(BF16)
        vt_ref[:, 0, :V_DIM] = v.T.reshape(n_h, V_DIM, tm).astype(BF16)
        ones_row = lax.broadcasted_iota(jnp.int32, (n_h, VT_PAD_ROWS, tm), 1) == 0
        vt_ref[:, 0, V_DIM:] = jnp.where(ones_row, 1.0, 0.0).astype(BF16)
    else:
        q_ref[...] = q


def _combine_kvq(x1, gates, y_rows, dest, p, *, for_attn, tm):
    t, d = x1.shape
    dest3 = dest.reshape(t // tm, 1, tm * TOP_K)
    args = [dest3, x1, gates, y_rows, p['g_kv'][None, :], p['w_k'].astype(BF16), p['w_v'].astype(BF16),
            p['g_mix'][N_A_LAYERS][None, :], p['b_w_q'][0].astype(BF16)]
    tile = pl.BlockSpec((tm, d), lambda i: (i, 0))
    in_specs = [pl.BlockSpec((1, 1, tm * TOP_K), lambda i: (i, 0, 0), memory_space=pltpu.SMEM),
                tile, pl.BlockSpec((tm, TOP_K), lambda i: (i, 0)), pl.BlockSpec(memory_space=pl.ANY)]
    in_specs += [_full(a.shape) for a in args[4:]]
    out_shape = [jax.ShapeDtypeStruct((t, d), F32)] * 3
    out_specs = [tile] * 3
    if for_attn:
        n_h = d // V7X_LANES
        out_shape += [jax.ShapeDtypeStruct((n_h, t // tm, V7X_LANES, tm), BF16), jax.ShapeDtypeStruct((t, d), BF16),
                      jax.ShapeDtypeStruct((n_h, t // tm, V_DIM + VT_PAD_ROWS, tm), BF16)]
        out_specs += [pl.BlockSpec((n_h, 1, V7X_LANES, tm), lambda i: (0, i, 0, 0)), tile,
                      pl.BlockSpec((n_h, 1, V_DIM + VT_PAD_ROWS, tm), lambda i: (0, i, 0, 0))]
    else:
        out_shape += [jax.ShapeDtypeStruct((t, d), F32)]
        out_specs += [tile]
    return pl.pallas_call(
        functools.partial(_combine_kvq_kernel, for_attn=for_attn),
        grid=(t // tm,), in_specs=in_specs, out_specs=out_specs, out_shape=out_shape,
        scratch_shapes=[pltpu.VMEM((TOP_K, tm, d), F32), pltpu.SemaphoreType.DMA(())],
        compiler_params=_cparams(1), name="moe_combine_kvq",
    )(*args)


def _bucket_np(n):
    n = np.asarray(n)
    nf = np.maximum(n, 1).astype(np.float32)
    large = MAX_EXACT + (np.log(nf / np.float32(MAX_EXACT)) / np.float32(math.log(MAX_DISTANCE / MAX_EXACT))
                         * np.float32(N_BUCKETS - MAX_EXACT)).astype(np.int32)
    return np.where(n < MAX_EXACT, n, np.minimum(large, N_BUCKETS - 1)).astype(np.int32)


def _bias_tiles_kernel(tbl_ref, bkt_ref, bkt_s_ref, tile_ref, last_ref, zero_ref):
    h = pl.program_id(0)
    far = tbl_ref[N_BUCKETS - 1, h]
    for dl in range(2):
        b = bkt_ref[dl]
        acc = jnp.where(b < 0, NEG_BIG, 0.0).astype(F32)
        for bb in range(N_BUCKETS - 1):
            acc = jnp.where(b == bb, (tbl_ref[bb, h] - far) * LOG2E, acc)
        tile_ref[0, dl] = acc

    @pl.when(h == 0)
    def _():
        last_ref[...] = jnp.zeros_like(last_ref)
        zero_ref[...] = jnp.zeros_like(zero_ref)

    bs = bkt_s_ref[...]
    val = jnp.zeros(bs.shape, F32)
    for bb in range(N_BUCKETS - 1):
        val = jnp.where(bs == bb, tbl_ref[bb, h] - far, val)
    row = lax.broadcasted_iota(jnp.int32, last_ref.shape, 0) % N_HEADS
    last_ref[...] = jnp.where(row == h, val, last_ref[...])
    zero_ref[...] = jnp.where(row == h, tbl_ref[0, h] - far, zero_ref[...])


def _bias_tiles(rel_table, tb, page):
    assert tb >= MAX_DISTANCE and page >= MAX_DISTANCE and page == V7X_LANES
    kq = np.arange(tb)
    tiles = []
    for dl in range(2):
        dist = dl * tb + kq[None, :] - kq[:, None]
        tiles.append(np.where(dist >= 0, _bucket_np(np.maximum(dist, 0)), -1))
    bkt = jnp.asarray(np.stack(tiles).astype(np.int32))
    bkt_s = jnp.asarray(_bucket_np(page - np.arange(page))[None, :].astype(np.int32))
    rows = (2 * N_HEADS, V7X_LANES)
    return pl.pallas_call(
        _bias_tiles_kernel,
        grid=(N_HEADS,),
        in_specs=[pl.BlockSpec(memory_space=pltpu.SMEM), _full(bkt.shape), _full(bkt_s.shape)],
        out_specs=[pl.BlockSpec((1, 2, tb, tb), lambda h: (h, 0, 0, 0)),
                   pl.BlockSpec(rows, lambda h: (0, 0)), pl.BlockSpec(rows, lambda h: (0, 0))],
        out_shape=[jax.ShapeDtypeStruct((N_HEADS, 2, tb, tb), F32),
                   jax.ShapeDtypeStruct(rows, F32), jax.ShapeDtypeStruct(rows, F32)],
        compiler_params=_cparams(1), name="rel_bias_tiles",
    )(rel_table, bkt, bkt_s)


def _attn_prompt_kernel(lam_ref, q_ref, k_ref, vt_ref, bias_ref, o_ref, acc_ref, m_ref, sa_ref, sb_ref,
                        pa_ref, pb_ref, *, tb):
    qi = pl.program_id(1)
    qt = q_ref[0, 0]
    row = lax.broadcasted_iota(jnp.int32, qt.shape, 0)
    zero = jnp.zeros_like(qt)
    qcat = jnp.concatenate([jnp.where(row < HEAD_DIM, qt, zero), jnp.where(row >= HEAD_DIM, qt, zero)], axis=1)
    acc_ref[...] = jnp.zeros_like(acc_ref)

    def scores(j, bias=None):
        k_blk = k_ref[pl.ds(pl.multiple_of(j * tb, tb), tb), :]
        s = jnp.dot(k_blk, qcat, preferred_element_type=F32)
        if bias is not None:
            s = s + jnp.concatenate([bias, bias], axis=1)
        return s, jnp.max(s, axis=0, keepdims=True)

    def absorb(s_ref, p_ref, s_max, j, m_old):
        m_new = jnp.maximum(m_old, s_max)
        for c in range(0, tb, ATTN_ROW_CHUNK):
            p_ref[c:c + ATTN_ROW_CHUNK, :] = jnp.exp2((s_ref[c:c + ATTN_ROW_CHUNK, :] - m_new).astype(BF16))
        acc_ref[...] = (jnp.exp2(m_old - m_new) * acc_ref[...]
                        + jnp.dot(vt_ref[0, j], p_ref[...], preferred_element_type=F32))
        return m_new

    n_far = jnp.maximum(qi - 1, 0)
    n_pairs = lax.shift_right_logical(n_far, 1)
    neg = jnp.full((1, 2 * tb), NEG_BIG, F32)
    m_ref[...] = neg

    @pl.when(n_pairs > 0)
    def _():
        sa_ref[...], m_ref[...] = scores(0)

    def pair(i, carry):
        m_old, max_a = carry
        a = 2 * i
        sb_ref[...], max_b = scores(a + 1)
        m_old = absorb(sa_ref, pa_ref, max_a, a, m_old)
        sa_ref[...], max_a = scores(a + 2)
        m_old = absorb(sb_ref, pb_ref, max_b, a + 1, m_old)
        return m_old, max_a

    m_ref[...], _ = lax.fori_loop(0, n_pairs, pair, (neg, m_ref[...]))

    def single(j, bias=None):
        sa_ref[...], s_max = scores(j, bias)
        m_ref[...] = absorb(sa_ref, pa_ref, s_max, j, m_ref[...])

    @pl.when(n_far > 2 * n_pairs)
    def _():
        single(2 * n_pairs)

    @pl.when(qi >= 1)
    def _():
        single(qi - 1, bias_ref[0, 1])
    single(qi, bias_ref[0, 0])
    o_all = acc_ref[:V_DIM, :] / acc_ref[V_DIM:V_DIM + 1, :]
    o_ref[...] = (o_all[:, :tb] - lam_ref[0] * o_all[:, tb:]).T


def _attn_prompt(qt4, kb, vt4, bias_tiles, lam, tb):
    t, d = kb.shape
    n_h = d // V7X_LANES
    n_q = t // tb
    vt_rows = vt4.shape[2]
    return pl.pallas_call(
        functools.partial(_attn_prompt_kernel, tb=tb),
        grid=(n_h, n_q),
        in_specs=[pl.BlockSpec(memory_space=pltpu.SMEM),
                  pl.BlockSpec((1, 1, V7X_LANES, tb), lambda h, i: (h, i, 0, 0)),
                  pl.BlockSpec((t, V7X_LANES), lambda h, i: (0, h)),
                  pl.BlockSpec((1, n_q, vt_rows, tb), lambda h, i: (h, 0, 0, 0)),
                  pl.BlockSpec((1, 2, tb, tb), lambda h, i: (h, 0, 0, 0))],
        out_specs=pl.BlockSpec((tb, V7X_LANES), lambda h, i: (i, h)),
        out_shape=jax.ShapeDtypeStruct((t, d), F32),
        scratch_shapes=[pltpu.VMEM((vt_rows, 2 * tb), F32), pltpu.VMEM((1, 2 * tb), F32),
                        pltpu.VMEM((tb, 2 * tb), F32), pltpu.VMEM((tb, 2 * tb), F32),
                        pltpu.VMEM((tb, 2 * tb), BF16), pltpu.VMEM((tb, 2 * tb), BF16)],
        compiler_params=_cparams(2), name="attn_prompt",
    )(lam, qt4, kb, vt4, bias_tiles)


def _attn_sample_kernel(pt_ref, lam_ref, qbd_ref, q_ref, kn_ref, vn_ref, hmask_ref, expand_ref, blast_ref,
                        bzero_ref, *rest, n_pp):
    k_refs = rest[:n_pp]
    v_refs = rest[n_pp:2 * n_pp]
    o_ref, acc_ref, m_ref, l_ref = rest[2 * n_pp:]
    del pt_ref
    g = pl.program_id(1)
    n_g = pl.num_programs(1)
    n_h = N_HEADS

    @pl.when(g == 0)
    def _():
        qv = q_ref[0].astype(BF16).astype(F32)
        s0 = jnp.sum(qv * kn_ref[0].astype(BF16).astype(F32), axis=-1, keepdims=True)
        m_ref[...] = s0 + bzero_ref[:, 0:1]
        l_ref[...] = jnp.ones_like(l_ref)
        vn = vn_ref[0].astype(BF16).astype(F32)
        acc_ref[...] = jnp.concatenate([vn, vn], axis=0)

    qbd = qbd_ref[0].astype(BF16)
    parts = []
    for i in range(n_pp):
        kst = k_refs[i][0]
        kst = kst.reshape(kst.shape[0] * kst.shape[1], kst.shape[2]).astype(BF16)
        parts.append(jnp.dot(qbd, kst, preferred_element_type=F32))
    parts[-1] = parts[-1] + jnp.where(g == n_g - 1, 1.0, 0.0) * blast_ref[...]
    s = jnp.concatenate(parts, axis=1)
    m_old = m_ref[...]
    m_new = jnp.maximum(m_old, jnp.max(s, axis=-1, keepdims=True))
    alpha = jnp.exp(m_old - m_new)
    pr = jnp.exp(s - m_new)
    l_ref[...] = alpha * l_ref[...] + jnp.sum(pr, axis=-1, keepdims=True)
    m_ref[...] = m_new
    keys = pr.shape[1] // n_pp
    n_m = pr.shape[0]
    p_rows = jnp.concatenate([pr[:, i * keys:(i + 1) * keys] for i in range(n_pp)], axis=0).astype(BF16)
    spread = jnp.dot(p_rows, expand_ref[...], preferred_element_type=F32) * hmask_ref[...]
    spread = spread.astype(BF16)
    acc = alpha * acc_ref[...]
    for i in range(n_pp):
        acc = acc + jnp.dot(spread[i * n_m:(i + 1) * n_m], v_refs[i][0].astype(BF16),
                            preferred_element_type=F32)
    acc_ref[...] = acc

    @pl.when(g == n_g - 1)
    def _():
        o = acc_ref[...] / l_ref[...]
        o_ref[0] = o[:n_h] - lam_ref[0] * o[n_h:]


def _attn_sample(q, k_new, v_new, cache_k, cache_v, page_table, blast, bzero, lam):
    nb, d = q.shape
    n_phys, page = cache_k.shape[0], cache_k.shape[1]
    n_pages = page_table.shape[1]
    n_pp = math.gcd(PAGES_PER_STEP, n_pages)
    n_h = N_HEADS
    n_m = 2 * n_h
    perm = np.concatenate([np.arange(0, n_m, 2), np.arange(1, n_m, 2)])
    qp = (q * ATTN_SCALE).reshape(nb, n_m, HEAD_DIM)[:, perm]
    kn = k_new.reshape(nb, n_m, HEAD_DIM)[:, perm]
    vn = v_new.reshape(nb, n_h, V_DIM)
    own = jnp.asarray((perm[:, None] == np.arange(n_m)[None, :])[None, :, :, None])
    qbd = jnp.where(own, qp[:, :, None, :], 0.0).reshape(nb, n_m, n_m * HEAD_DIM)
    ckt = jnp.transpose(cache_k, (0, 2, 3, 1))
    cv = cache_v.reshape(n_phys, page * n_h, V_DIM)
    lanes = np.arange(page * n_h)
    hmask = jnp.asarray((lanes[None, :] % n_h == np.arange(n_pp * n_m)[:, None] % n_h).astype(np.float32))
    expand = jnp.asarray((lanes[None, :] // n_h == np.arange(page)[:, None]).astype(np.float32), dtype=BF16)

    def page_map(i):
        return lambda b, g, pt: (pt[b * n_pages + g * n_pp + i], 0, 0, 0)

    def page_map3(i):
        return lambda b, g, pt: (pt[b * n_pages + g * n_pp + i], 0, 0)
    row3 = lambda b, g, pt: (b, 0, 0)
    const2 = lambda b, g, pt: (0, 0)
    in_specs = [pl.BlockSpec(memory_space=pltpu.SMEM),
                pl.BlockSpec((1, n_m, n_m * HEAD_DIM), row3),
                pl.BlockSpec((1, n_m, HEAD_DIM), row3), pl.BlockSpec((1, n_m, HEAD_DIM), row3),
                pl.BlockSpec((1, n_h, V_DIM), row3), pl.BlockSpec(hmask.shape, const2),
                pl.BlockSpec(expand.shape, const2), pl.BlockSpec(blast.shape, const2),
                pl.BlockSpec(bzero.shape, const2)]
    in_specs += [pl.BlockSpec((1, n_m, HEAD_DIM, page), page_map(i)) for i in range(n_pp)]
    in_specs += [pl.BlockSpec((1, page * n_h, V_DIM), page_map3(i)) for i in range(n_pp)]
    out = pl.pallas_call(
        functools.partial(_attn_sample_kernel, n_pp=n_pp),
        grid_spec=pltpu.PrefetchScalarGridSpec(
            num_scalar_prefetch=1, grid=(nb, n_pages // n_pp), in_specs=in_specs,
            out_specs=pl.BlockSpec((1, n_h, V_DIM), row3),
            scratch_shapes=[pltpu.VMEM((n_m, V_DIM), F32), pltpu.VMEM((n_m, 1), F32),
                            pltpu.VMEM((n_m, 1), F32)]),
        out_shape=jax.ShapeDtypeStruct((nb, n_h, V_DIM), F32),
        compiler_params=_cparams(2), name="attn_sample",
    )(page_table.reshape(-1).astype(jnp.int32), lam, qbd, qp, kn, vn, hmask, expand, blast, bzero,
      *([ckt] * n_pp), *([cv] * n_pp))
    return out.reshape(nb, d)


def _attn_out_kernel(o_ref, x2_ref, gsub_ref, wo_ref, gffn_ref, wr_ref, br_ref,
                     x3_ref, xn_ref, tope_ref, gate_ref, rank_ref, cnt_ref, runcnt_ref, on_ref,
                     *, sub_scale):
    @pl.when(pl.program_id(0) == 0)
    def _():
        runcnt_ref[...] = jnp.zeros_like(runcnt_ref)

    vd = gsub_ref.shape[1]
    for h in range(o_ref.shape[1] // vd):
        oh = o_ref[:, h * vd:(h + 1) * vd]
        on_ref[:, h * vd:(h + 1) * vd] = _rms(oh, gsub_ref[...]) * sub_scale
    x3 = x2_ref[...] + _mm(on_ref[...], wo_ref[...])
    x3_ref[...] = x3
    _router_tail(x3, gffn_ref[...], wr_ref[...], br_ref[...], runcnt_ref,
                 xn_ref, tope_ref, gate_ref, rank_ref, cnt_ref)


def _attn_out(o, x2, p, l, sub_scale, *, tm):
    t, d = x2.shape
    n_e = p['moe_w_router'].shape[-1]
    args = [o, x2, p['b_g_sub'][0][None, :], p['b_w_o'][0].astype(BF16), p['g_ffn'][l][None, :],
            p['moe_w_router'][l].astype(BF16), p['moe_b_router'][l][None, :]]
    tile = pl.BlockSpec((tm, d), lambda i: (i, 0))
    r_shapes, r_specs = _router_out_specs(t, tm, d, n_e)
    outs = pl.pallas_call(
        functools.partial(_attn_out_kernel, sub_scale=sub_scale),
        grid=(t // tm,), in_specs=[tile, tile] + [_full(a.shape) for a in args[2:]],
        out_specs=[tile, tile] + r_specs,
        out_shape=[jax.ShapeDtypeStruct((t, d), F32)] * 2 + r_shapes,
        scratch_shapes=[pltpu.VMEM((1, n_e), F32), pltpu.VMEM((tm, d), F32)],
        compiler_params=_cparams(1), name="attn_out_router",
    )(*args)
    return outs


def _combine_final_kernel(dest_ref, x3_ref, gate_ref, y_hbm, gout_ref, y_ref, ybuf, sem):
    tm = x3_ref.shape[0]
    x4 = x3_ref[...] + _gather_combine(dest_ref, y_hbm, ybuf, sem, gate_ref[...], tm)
    y_ref[...] = _rms(x4, gout_ref[...])


def _combine_final(x3, gates, y_rows, dest, g_out, tm):
    t, d = x3.shape
    dest3 = dest.reshape(t // tm, 1, tm * TOP_K)
    tile = pl.BlockSpec((tm, d), lambda i: (i, 0))
    return pl.pallas_call(
        _combine_final_kernel,
        grid=(t // tm,),
        in_specs=[pl.BlockSpec((1, 1, tm * TOP_K), lambda i: (i, 0, 0), memory_space=pltpu.SMEM),
                  tile, pl.BlockSpec((tm, TOP_K), lambda i: (i, 0)), pl.BlockSpec(memory_space=pl.ANY),
                  _full((1, d))],
        out_specs=tile, out_shape=jax.ShapeDtypeStruct((t, d), F32),
        scratch_shapes=[pltpu.VMEM((TOP_K, tm, d), F32), pltpu.SemaphoreType.DMA(())],
        compiler_params=_cparams(1), name="moe_combine_final",
    )(dest3, x3, gates, y_rows, g_out[None, :])


def kernel(x_prompt, x_sample, cache_k, cache_v, page_table, g_mix, g_ffn, g_kv, g_out, a_w_in, a_b_in, a_g_v, a_b_v, a_w_s, a_b_s, a_w_out, w_k, w_v, b_w_q, b_lq1, b_lk1, b_lq2, b_lk2, b_g_sub, b_w_o, rel_table, moe_w_router, moe_b_router, moe_w_gu, moe_b_gu, moe_w_down, moe_b_down):
    p = {
        'g_mix': g_mix, 'g_ffn': g_ffn, 'g_kv': g_kv, 'g_out': g_out,
        'a_w_in': a_w_in, 'a_b_in': a_b_in, 'a_g_v': a_g_v, 'a_b_v': a_b_v,
        'a_w_s': a_w_s, 'a_b_s': a_b_s, 'a_w_out': a_w_out,
        'w_k': w_k, 'w_v': w_v, 'b_w_q': b_w_q, 'b_g_sub': b_g_sub, 'b_w_o': b_w_o,
        'moe_w_router': moe_w_router, 'moe_b_router': moe_b_router,
        'moe_w_gu': moe_w_gu, 'moe_b_gu': moe_b_gu, 'moe_w_down': moe_w_down, 'moe_b_down': moe_b_down,
    }
    assert x_prompt.shape[0] == 1 and x_sample.shape[1] == 1
    b, s, d = x_prompt.shape
    nb = x_sample.shape[0]
    page = cache_k.shape[1]
    lam_init = 0.8 - 0.6 * math.exp(-0.3 * N_A_LAYERS)
    lam = (jnp.exp(jnp.sum(b_lq1[0] * b_lk1[0])) - jnp.exp(jnp.sum(b_lq2[0] * b_lk2[0])) + lam_init)
    lam = lam.astype(F32)[None]
    tb = min(ATTN_BLOCK, s)
    bias_tiles, blast, bzero = _bias_tiles(rel_table, tb, page)
    tm_p, tm_s = PROMPT_TOKEN_TILE, nb
    assert s % tm_p == 0 and tm_p % CHUNK == 0 and s % tb == 0

    x1_p, xn_p, _, tope_p, gates_p, rank_p, cnt_p = _mixer_a(
        x_prompt[0], p, 0, chunked=True, want_v=False, tm=tm_p)
    x1_s, xn_s, cv_s, tope_s, gates_s, rank_s, cnt_s = _mixer_a(
        x_sample[:, 0], p, 0, chunked=False, want_v=True, tm=tm_s)
    y_rows, (dest_p, dest_s), rows_buf = _moe_rows(
        [(xn_p, tope_p, rank_p, cnt_p, tm_p), (xn_s, tope_s, rank_s, cnt_s, tm_s)], p, 0)
    x2_p, k_p, v_p, qt4, kb, vt4 = _combine_kvq(x1_p, gates_p, y_rows, dest_p, p, for_attn=True, tm=tb)
    x2_s, k_s, v_s, q_s = _combine_kvq(x1_s, gates_s, y_rows, dest_s, p, for_attn=False, tm=tm_s)

    o_p = _attn_prompt(qt4, kb, vt4, bias_tiles, lam, tb)
    o_s = _attn_sample(q_s, k_s, v_s, cache_k, cache_v, page_table, blast, bzero, lam)
    x3_p, xn_p, tope_p, gates_p, rank_p, cnt_p = _attn_out(o_p, x2_p, p, 1, 1.0 - lam_init, tm=tm_p)
    x3_s, xn_s, tope_s, gates_s, rank_s, cnt_s = _attn_out(o_s, x2_s, p, 1, 1.0 - lam_init, tm=tm_s)
    y_rows, (dest_p, dest_s), _ = _moe_rows(
        [(xn_p, tope_p, rank_p, cnt_p, tm_p), (xn_s, tope_s, rank_s, cnt_s, tm_s)], p, 1, rows_buf)
    y_p = _combine_final(x3_p, gates_p, y_rows, dest_p, g_out, tm_p)
    y_s = _combine_final(x3_s, gates_s, y_rows, dest_s, g_out, tm_s)
    n_h = N_HEADS
    return (y_p[None], y_s[:, None],
            k_p.reshape(1, s, 2 * n_h, HEAD_DIM), v_p.reshape(1, s, n_h, V_DIM),
            k_s.reshape(nb, 1, 2 * n_h, HEAD_DIM), v_s.reshape(nb, 1, n_h, V_DIM),
            cv_s.reshape(1, nb, 1, -1))
```
